```python
import jax, jax.numpy as jnp
from jax import lax
import numpy as np

D_MODEL = 2048
BATCH = 4
SEQ = 8192
DEPTH = 2

N_MIXERS = 2
MIX_WIDTH = D_MODEL
FOX_HEADS = 16
FOX_HEAD_DIM = MIX_WIDTH // FOX_HEADS
Q_BLOCK = 128
HGRN_HEADS = 16
HGRN_KEY_DIM = MIX_WIDTH // HGRN_HEADS
HGRN_VAL_DIM = MIX_WIDTH // HGRN_HEADS
HGRN_CHUNK = 64
N_FOX_LAYERS = (DEPTH + 1) // 2
N_HGRN_LAYERS = DEPTH // 2
FOX_IN = 4 * MIX_WIDTH + FOX_HEADS
HGRN_IN = 4 * MIX_WIDTH
EPS = 1e-6

kernel_name = "fox_hgrn2_interleaved_hybrid"


def rms_norm(x, gain):
    xf = x.astype(jnp.float32)
    y = xf * lax.rsqrt(jnp.mean(xf * xf, axis=-1, keepdims=True) + EPS)
    return y.astype(x.dtype) * gain


def split_heads(t, n_heads):
    b, s, _ = t.shape
    return t.reshape(b, s, n_heads, -1).transpose(0, 2, 1, 3)


def fox_mixer(h, w_in, b_f):
    B, S, _ = h.shape
    W, H, dh = MIX_WIDTH, FOX_HEADS, FOX_HEAD_DIM
    proj = h @ w_in
    q = split_heads(proj[..., :W], H)
    k = split_heads(proj[..., W:2 * W], H)
    v = split_heads(proj[..., 2 * W:3 * W], H)
    f_logit = proj[..., 3 * W:3 * W + H]
    gate = proj[..., 3 * W + H:]
    log_f = jax.nn.log_sigmoid((f_logit + b_f).astype(jnp.float32)).transpose(0, 2, 1)
    c = jnp.cumsum(log_f, axis=-1)
    nq = S // Q_BLOCK
    qb = q.reshape(B, H, nq, Q_BLOCK, dh).transpose(2, 0, 1, 3, 4)
    cb = c.reshape(B, H, nq, Q_BLOCK).transpose(2, 0, 1, 3)
    starts = jnp.arange(nq) * Q_BLOCK
    key_pos = jnp.arange(S)
    scale = dh ** -0.5

    def block(args):
        q_blk, c_blk, start = args
        logits = jnp.einsum('bhqd,bhkd->bhqk', q_blk, k).astype(jnp.float32) * scale
        logits = logits + (c_blk[..., :, None] - c[:, :, None, :])
        q_pos = start + jnp.arange(Q_BLOCK)
        causal = key_pos[None, :] <= q_pos[:, None]
        logits = jnp.where(causal, logits, -jnp.inf)
        p = jax.nn.softmax(logits, axis=-1).astype(v.dtype)
        return jnp.einsum('bhqk,bhkd->bhqd', p, v)

    o = lax.map(block, (qb, cb, starts))
    o = o.transpose(1, 0, 3, 2, 4).reshape(B, S, W)
    return o * jax.nn.silu(gate)


def hgrn2_mixer(h, w_in, lb, onorm_gain):
    B, S, _ = h.shape
    H, dk, dv, C = HGRN_HEADS, HGRN_KEY_DIM, HGRN_VAL_DIM, HGRN_CHUNK
    proj = h @ w_in
    q_raw, f_raw, i_raw, gate = jnp.split(proj, 4, axis=-1)
    q = split_heads(jax.nn.silu(q_raw), H).astype(jnp.float32)
    fz = split_heads(f_raw, H).astype(jnp.float32)
    v = split_heads(i_raw, H).astype(jnp.float32)
    lb_h = lb.astype(jnp.float32).reshape(H, 1, dk)
    log_f = jnp.log(lb_h + (1.0 - lb_h) * jax.nn.sigmoid(fz))
    k = (1.0 - lb_h) * jax.nn.sigmoid(-fz)
    nc = S // C

    def to_chunks(t):
        return t.reshape(B, H, nc, C, t.shape[-1]).transpose(2, 0, 1, 3, 4)

    tri = jnp.tril(jnp.ones((C, C), dtype=bool))

    def step(state, inp):
        q_c, k_c, lf_c, v_c = inp
        b = jnp.cumsum(lf_c, axis=-2)
        b_last = b[:, :, -1, :]
        inter = jnp.einsum('bhtd,bhde->bhte', q_c * jnp.exp(b), state)
        rel = jnp.where(tri[:, :, None], b[:, :, :, None, :] - b[:, :, None, :, :], -jnp.inf)
        A = jnp.einsum('bhtd,bhsd,bhtsd->bhts', q_c, k_c, jnp.exp(rel))
        intra = jnp.einsum('bhts,bhse->bhte', A, v_c)
        new_state = jnp.exp(b_last)[..., None] * state + jnp.einsum(
            'bhsd,bhse->bhde', k_c * jnp.exp(b_last[:, :, None, :] - b), v_c)
        return new_state, inter + intra

    state0 = jnp.zeros((B, H, dk, dv), jnp.float32)
    _, o = lax.scan(step, state0, (to_chunks(q), to_chunks(k), to_chunks(log_f), to_chunks(v)))
    o = o.transpose(1, 2, 0, 3, 4).reshape(B, H, S, dv)
    o = o * lax.rsqrt(jnp.mean(o * o, axis=-1, keepdims=True) + EPS)
    o = o.transpose(0, 2, 1, 3).reshape(B, S, MIX_WIDTH).astype(h.dtype) * onorm_gain
    return o * jax.nn.silu(gate)


def setup_inputs(seed: int = 0) -> dict:
    key = jax.random.key(seed)
    ks = jax.random.split(key, 10)
    f32 = jnp.float32
    x = jax.random.normal(ks[0], (BATCH, SEQ, D_MODEL), f32)
    norm_gains = 1.0 + 0.02 * jax.random.normal(ks[1], (DEPTH, D_MODEL), f32)
    fox_w_in = jax.random.normal(ks[2], (N_FOX_LAYERS, D_MODEL, FOX_IN), f32) * D_MODEL ** -0.5
    fox_b_f = 1.0 + 0.1 * jax.random.normal(ks[3], (N_FOX_LAYERS, FOX_HEADS), f32)
    hgrn_w_in = jax.random.normal(ks[4], (N_HGRN_LAYERS, D_MODEL, HGRN_IN), f32) * D_MODEL ** -0.5
    hgrn_lb_logits = 0.5 * jax.random.normal(ks[5], (DEPTH, MIX_WIDTH), f32)
    hgrn_onorm = 1.0 + 0.02 * jax.random.normal(ks[6], (N_HGRN_LAYERS, MIX_WIDTH), f32)
    w_out = jax.random.normal(ks[7], (DEPTH, MIX_WIDTH, D_MODEL), f32) * MIX_WIDTH ** -0.5
    final_gain = 1.0 + 0.02 * jax.random.normal(ks[8], (D_MODEL,), f32)
    return {"x": x, "norm_gains": norm_gains, "fox_w_in": fox_w_in, "fox_b_f": fox_b_f,
            "hgrn_w_in": hgrn_w_in, "hgrn_lb_logits": hgrn_lb_logits, "hgrn_onorm": hgrn_onorm,
            "w_out": w_out, "final_gain": final_gain}


def reference(x, norm_gains, fox_w_in, fox_b_f, hgrn_w_in, hgrn_lb_logits, hgrn_onorm, w_out, final_gain):
    lb_all = jnp.cumsum(jax.nn.softmax(hgrn_lb_logits.astype(jnp.float32), axis=0), axis=0)
    lb_all = lb_all - lb_all[0:1]
    for i in range(DEPTH):
        h = rms_norm(x, norm_gains[i])
        j = i // N_MIXERS
        if i % N_MIXERS == 0:
            y = fox_mixer(h, fox_w_in[j], fox_b_f[j])
        else:
            y = hgrn2_mixer(h, hgrn_w_in[j], lb_all[i], hgrn_onorm[j])
        x = x + y @ w_out[i]
    return rms_norm(x, final_gain)
```

```python
import functools

import numpy as np
import jax
import jax.numpy as jnp
from jax import lax
from jax.experimental import pallas as pl
from jax.experimental.pallas import tpu as pltpu

EPS = 1e-6
N_HEADS = 16
LANES = 128
HGRN_CHUNK = 128
VMEM_LIMIT = 56 * 1024 * 1024

_F32 = jnp.float32
_BF16 = jnp.bfloat16
_NT = (((1,), (1,)), ((), ()))


def _silu(x):
    return x / (1.0 + jnp.exp(-x))


def _split_bf16(x, terms):
    pieces = []
    rem = x
    for _ in range(terms):
        p = rem.astype(_BF16)
        pieces.append(p)
        rem = rem - p.astype(_F32)
    return pieces


def _params(semantics):
    return pltpu.CompilerParams(dimension_semantics=semantics, vmem_limit_bytes=VMEM_LIMIT)


def _proj_kernel(*refs, apply_norm, n_groups, epilogues, has_extra, q_scale):
    it = iter(refs)
    x_ref = next(it)
    gain_ref = next(it) if apply_norm else None
    w_refs = [next(it) for _ in range(n_groups)]
    wx_ref = next(it) if has_extra else None
    o_refs = [next(it) for _ in range(n_groups)]
    ox_ref = next(it) if has_extra else None
    h_scr = next(it) if apply_norm else None

    j = pl.program_id(1)

    if apply_norm:
        @pl.when(j == 0)
        def _():
            xf = x_ref[...].astype(_F32)
            y = xf * lax.rsqrt(jnp.mean(xf * xf, axis=-1, keepdims=True) + EPS)
            h_scr[...] = (y * gain_ref[...]).astype(_BF16)
            if has_extra:
                ox_ref[...] = jnp.dot(h_scr[...], wx_ref[...], preferred_element_type=_F32)
        h = h_scr[...]
    else:
        h = x_ref[...]
        if has_extra:
            @pl.when(j == 0)
            def _():
                ox_ref[...] = jnp.dot(h, wx_ref[...], preferred_element_type=_F32)

    for g in range(n_groups):
        acc = jnp.dot(h, w_refs[g][...], preferred_element_type=_F32)
        if epilogues[g] == "scale":
            acc = acc * q_scale
        elif epilogues[g] == "silu":
            acc = _silu(acc)
        o_refs[g][...] = acc.astype(o_refs[g].dtype)


def _proj_call(x, gain, weights, extra_w, epilogues, out_dtypes, *, q_scale=1.0, tm=512, tn=512, name):
    m, d = x.shape
    n = weights[0].shape[1]
    tm, tn = min(tm, m), min(tn, n)
    assert m % tm == 0 and n % tn == 0
    apply_norm = gain is not None
    has_extra = extra_w is not None
    n_groups = len(weights)

    in_specs = [pl.BlockSpec((tm, d), lambda i, j: (i, 0))]
    args = [x]
    if apply_norm:
        in_specs.append(pl.BlockSpec((1, d), lambda i, j: (0, 0)))
        args.append(gain)
    for w in weights:
        in_specs.append(pl.BlockSpec((d, tn), lambda i, j: (0, j)))
        args.append(w)
    out_specs = [pl.BlockSpec((tm, tn), lambda i, j: (i, j)) for _ in weights]
    out_shape = [jax.ShapeDtypeStruct((m, n), dt) for dt in out_dtypes]
    if has_extra:
        in_specs.append(pl.BlockSpec((d, LANES), lambda i, j: (0, 0)))
        args.append(extra_w)
        out_specs.append(pl.BlockSpec((tm, LANES), lambda i, j: (i, 0)))
        out_shape.append(jax.ShapeDtypeStruct((m, LANES), _F32))
    scratch = [pltpu.VMEM((tm, d), _BF16)] if apply_norm else []

    kern = functools.partial(_proj_kernel, apply_norm=apply_norm, n_groups=n_groups,
                             epilogues=tuple(epilogues), has_extra=has_extra, q_scale=q_scale)
    return pl.pallas_call(
        kern,
        grid=(m // tm, n // tn),
        in_specs=in_specs,
        out_specs=out_specs,
        out_shape=out_shape,
        scratch_shapes=scratch,
        compiler_params=_params(("parallel", "arbitrary")),
        name=name,
    )(*args)


def _decay_kernel(f_ref, b_ref, c_ref, carry):
    t = pl.program_id(1)

    @pl.when(t == 0)
    def _():
        carry[...] = jnp.zeros_like(carry)

    z = f_ref[0] + b_ref[...]
    lf = jnp.minimum(z, 0.0) - jnp.log(1.0 + jnp.exp(-jnp.abs(z)))
    n = lf.shape[0]
    row = lax.broadcasted_iota(jnp.int32, (n, n), 0)
    col = lax.broadcasted_iota(jnp.int32, (n, n), 1)
    tri = (col <= row).astype(_BF16)
    cs = carry[0:1, :]
    for piece in _split_bf16(lf, 3):
        cs = cs + jnp.dot(tri, piece, preferred_element_type=_F32)
    c_ref[0] = cs
    carry[0:1, :] = cs[n - 1:n, :]


def _decay_call(flog, b_pad, *, tb=512):
    b, s, _ = flog.shape
    tb = min(tb, s)
    assert s % tb == 0
    return pl.pallas_call(
        _decay_kernel,
        grid=(b, s // tb),
        in_specs=[pl.BlockSpec((1, tb, LANES), lambda bi, ti: (bi, ti, 0)),
                  pl.BlockSpec((1, LANES), lambda bi, ti: (0, 0))],
        out_specs=pl.BlockSpec((1, tb, LANES), lambda bi, ti: (bi, ti, 0)),
        out_shape=jax.ShapeDtypeStruct((b, s, LANES), _F32),
        scratch_shapes=[pltpu.VMEM((8, LANES), _F32)],
        compiler_params=_params(("parallel", "arbitrary")),
        name="fox_decay_cumsum",
    )(flog, b_pad)


def _fox_attn_kernel(q_ref, k_ref, v_ref, g_ref, cq_ref, ck_ref, o_ref,
                     kaug, qaug, m_scr, l_scr, acc_scr, *, tq, kb_rows):
    h = pl.program_id(1)
    qi = pl.program_id(2)
    s_len = k_ref.shape[1]
    lane = lax.broadcasted_iota(jnp.int32, (1, LANES), 1)

    def head_column(c):
        return jnp.sum(jnp.where(lane == h, c, 0.0), axis=1, keepdims=True)

    @pl.when(qi == 0)
    def _():
        def build(r, carry):
            rows = pl.ds(pl.multiple_of(r * kb_rows, kb_rows), kb_rows)
            hi, mid, lo = [p.astype(_F32) for p in _split_bf16(head_column(ck_ref[0, rows, :]), 3)]
            aug = jnp.where(lane == 0, -hi, jnp.where(lane == 1, -mid, jnp.where(
                lane == 2, -lo, jnp.where(lane < 6, 1.0, 0.0))))
            kaug[rows, 0:LANES] = k_ref[0, rows, :]
            kaug[rows, LANES:2 * LANES] = aug.astype(_BF16)
            return carry
        lax.fori_loop(0, s_len // kb_rows, build, 0)

    hi, mid, lo = [p.astype(_F32) for p in _split_bf16(head_column(cq_ref[0]), 3)]
    qaug[:, 0:LANES] = q_ref[0]
    qaug[:, LANES:2 * LANES] = jnp.where(lane < 3, 1.0, jnp.where(lane == 3, hi, jnp.where(
        lane == 4, mid, jnp.where(lane == 5, lo, 0.0)))).astype(_BF16)
    m_scr[...] = jnp.full_like(m_scr, -jnp.inf)
    l_scr[...] = jnp.zeros_like(l_scr)
    acc_scr[...] = jnp.zeros_like(acc_scr)

    def step(j, masked):
        rows = pl.ds(pl.multiple_of(j * tq, tq), tq)
        s = lax.dot_general(qaug[...], kaug[rows, :], _NT, preferred_element_type=_F32)
        if masked:
            r = lax.broadcasted_iota(jnp.int32, s.shape, 0)
            c = lax.broadcasted_iota(jnp.int32, s.shape, 1)
            s = jnp.where(c <= r, s, -jnp.inf)
        m_prev = m_scr[...]
        m_new = jnp.maximum(m_prev, jnp.max(s, axis=1, keepdims=True))
        alpha = jnp.exp(m_prev - m_new)
        p = jnp.exp(s - m_new)
        l_scr[...] = alpha * l_scr[...] + jnp.sum(p, axis=1, keepdims=True)
        acc_scr[...] = alpha * acc_scr[...] + jnp.dot(
            p.astype(_BF16), v_ref[0, rows, :], preferred_element_type=_F32)
        m_scr[...] = m_new

    def full_step(j, carry):
        step(j, masked=False)
        return carry

    lax.fori_loop(0, qi, full_step, 0)
    step(qi, masked=True)

    o = acc_scr[...] / l_scr[...]
    o_ref[0] = (o * _silu(g_ref[0].astype(_F32))).astype(o_ref.dtype)


def _fox_attn_call(q, k, v, g, c, *, tq=512):
    b, s, w = q.shape
    nh = w // LANES
    tq = min(tq, s)
    assert s % tq == 0
    kern = functools.partial(_fox_attn_kernel, tq=tq, kb_rows=tq)
    head_tile = lambda bi, hi, qi: (bi, qi, hi)
    head_full = lambda bi, hi, qi: (bi, 0, hi)
    return pl.pallas_call(
        kern,
        grid=(b, nh, s // tq),
        in_specs=[pl.BlockSpec((1, tq, LANES), head_tile),
                  pl.BlockSpec((1, s, LANES), head_full),
                  pl.BlockSpec((1, s, LANES), head_full),
                  pl.BlockSpec((1, tq, LANES), head_tile),
                  pl.BlockSpec((1, tq, LANES), lambda bi, hi, qi: (bi, qi, 0)),
                  pl.BlockSpec((1, s, LANES), lambda bi, hi, qi: (bi, 0, 0))],
        out_specs=pl.BlockSpec((1, tq, LANES), head_tile),
        out_shape=jax.ShapeDtypeStruct((b, s, w), _BF16),
        scratch_shapes=[pltpu.VMEM((s, 2 * LANES), _BF16),
                        pltpu.VMEM((tq, 2 * LANES), _BF16),
                        pltpu.VMEM((tq, 1), _F32),
                        pltpu.VMEM((tq, 1), _F32),
                        pltpu.VMEM((tq, LANES), _F32)],
        compiler_params=_params(("parallel", "parallel", "arbitrary")),
        name="fox_attention",
    )(q, k, v, g, c, c)


def _out_proj_kernel(y_ref, w_ref, x_ref, gain_ref, *o_refs, emit_residual):
    xn = x_ref[...] + jnp.dot(y_ref[...], w_ref[...], preferred_element_type=_F32)
    normed = xn * lax.rsqrt(jnp.mean(xn * xn, axis=-1, keepdims=True) + EPS) * gain_ref[...]
    if emit_residual:
        o_refs[0][...] = xn
        o_refs[1][...] = normed.astype(o_refs[1].dtype)
    else:
        o_refs[0][...] = normed.astype(o_refs[0].dtype)


def _out_proj_call(y, w, x, gain, *, emit_residual, norm_dtype, tm=256, name):
    m, wd = y.shape
    d = w.shape[1]
    tm = min(tm, m)
    assert m % tm == 0
    row_tile = lambda i: (i, 0)
    out_specs = [pl.BlockSpec((tm, d), row_tile)]
    out_shape = [jax.ShapeDtypeStruct((m, d), norm_dtype)]
    if emit_residual:
        out_specs.insert(0, pl.BlockSpec((tm, d), row_tile))
        out_shape.insert(0, jax.ShapeDtypeStruct((m, d), _F32))
    return pl.pallas_call(
        functools.partial(_out_proj_kernel, emit_residual=emit_residual),
        grid=(m // tm,),
        in_specs=[pl.BlockSpec((tm, wd), row_tile),
                  pl.BlockSpec((wd, d), lambda i: (0, 0)),
                  pl.BlockSpec((tm, d), row_tile),
                  pl.BlockSpec((1, d), lambda i: (0, 0))],
        out_specs=out_specs,
        out_shape=out_shape,
        compiler_params=_params(("parallel",)),
        name=name,
    )(y, w, x, gain)


def _hgrn_constants(c):
    t = np.arange(c)[:, None]
    r = np.arange(c)[None, :]
    sums, masks = [], [np.eye(c, dtype=bool)]
    m = 1
    while m < c:
        same = (t // m) == (r // m)
        upper = ((t // m) % 2) == 1
        sums.append(same & np.where(upper, r <= t, r > t))
        masks.append(upper & ((r // m) == (t // m) - 1))
        m *= 2
    sums.append(r <= t)
    return (np.concatenate(sums, 0).astype(np.float32), np.stack(masks).astype(np.float32))


def _hgrn_kernel(q_ref, fz_ref, v_ref, g_ref, lbl_ref, on_ref, sums_ref, masks_ref, o_ref,
                 state, *, layer, chunk):
    si = pl.program_id(2)
    n_levels = masks_ref.shape[0]
    tb = q_ref.shape[1]

    @pl.when(si == 0)
    def _():
        state[...] = jnp.zeros_like(state)

    logits = lbl_ref[...].astype(_F32)
    e = jnp.exp(logits - jnp.max(logits, axis=0, keepdims=True))
    sm = e / jnp.sum(e, axis=0, keepdims=True)
    lb = jnp.sum(sm[1:layer + 1, :], axis=0, keepdims=True) if layer >= 1 else jnp.zeros_like(sm[0:1, :])
    gain = on_ref[...]

    def chunk_step(ci, carry):
        rows = pl.ds(pl.multiple_of(ci * chunk, chunk), chunk)
        q = q_ref[0, rows, :].astype(_F32)
        fz = fz_ref[0, rows, :]
        v = v_ref[0, rows, :]
        ez = jnp.exp(-jnp.abs(fz))
        rz = 1.0 / (1.0 + ez)
        pos = fz >= 0.0
        sig = jnp.where(pos, rz, ez * rz)
        nsig = jnp.where(pos, ez * rz, rz)
        lf = jnp.log(lb + (1.0 - lb) * sig)
        kk = (1.0 - lb) * nsig

        lf_hi, lf_lo = _split_bf16(lf, 2)
        d_all = jnp.dot(sums_ref[...], jnp.concatenate([lf_hi, lf_lo], axis=1),
                        preferred_element_type=_F32)
        d_all = d_all[:, 0:LANES] + d_all[:, LANES:2 * LANES]

        q_b = q.astype(_BF16)
        a = lax.dot_general(q_b, kk.astype(_BF16), _NT, preferred_element_type=_F32) * masks_ref[0]
        for lvl in range(1, n_levels):
            dec = jnp.exp(d_all[(lvl - 1) * chunk:lvl * chunk, :])
            a_l = lax.dot_general((q * dec).astype(_BF16), (kk * dec).astype(_BF16), _NT,
                                  preferred_element_type=_F32)
            a = a + a_l * masks_ref[lvl]

        bcum = d_all[(n_levels - 1) * chunk:n_levels * chunk, :]
        b_last = bcum[chunk - 1:chunk, :]
        st = state[...]
        inter = lax.dot_general((q * jnp.exp(bcum)).astype(_BF16), st.astype(_BF16), _NT,
                                preferred_element_type=_F32)
        o = inter + jnp.dot(a.astype(_BF16), v, preferred_element_type=_F32)

        k_dec = (kk * jnp.exp(b_last - bcum)).astype(_BF16)
        v_t = v.astype(_F32).T.astype(_BF16)
        state[...] = st * jnp.exp(b_last) + jnp.dot(v_t, k_dec, preferred_element_type=_F32)

        o = o * lax.rsqrt(jnp.mean(o * o, axis=-1, keepdims=True) + EPS)
        y = o * gain * _silu(g_ref[0, rows, :].astype(_F32))
        o_ref[0, rows, :] = y.astype(o_ref.dtype)
        return carry

    lax.fori_loop(0, tb // chunk, chunk_step, 0)


def _hgrn_call(q, fz, v, g, lb_logits, onorm, *, layer, tb=1024):
    b, s, w = q.shape
    nh = w // LANES
    tb = min(tb, s)
    chunk = min(HGRN_CHUNK, tb)
    assert s % tb == 0 and tb % chunk == 0
    sums, masks = _hgrn_constants(chunk)
    sums = jnp.asarray(sums, _BF16)
    masks = jnp.asarray(masks, _F32)
    depth = lb_logits.shape[0]
    head_tile = lambda bi, hi, si: (bi, si, hi)
    return pl.pallas_call(
        functools.partial(_hgrn_kernel, layer=layer, chunk=chunk),
        grid=(b, nh, s // tb),
        in_specs=[pl.BlockSpec((1, tb, LANES), head_tile),
                  pl.BlockSpec((1, tb, LANES), head_tile),
                  pl.BlockSpec((1, tb, LANES), head_tile),
                  pl.BlockSpec((1, tb, LANES), head_tile),
                  pl.BlockSpec((depth, LANES), lambda bi, hi, si: (0, hi)),
                  pl.BlockSpec((1, LANES), lambda bi, hi, si: (0, hi)),
                  pl.BlockSpec(sums.shape, lambda bi, hi, si: (0, 0)),
                  pl.BlockSpec(masks.shape, lambda bi, hi, si: (0, 0, 0))],
        out_specs=pl.BlockSpec((1, tb, LANES), head_tile),
        out_shape=jax.ShapeDtypeStruct((b, s, w), _BF16),
        scratch_shapes=[pltpu.VMEM((LANES, LANES), _F32)],
        compiler_params=_params(("parallel", "parallel", "arbitrary")),
        name="hgrn2_recurrence",
    )(q, fz, v, g, lb_logits, onorm, sums, masks)


def kernel(x, norm_gains, fox_w_in, fox_b_f, hgrn_w_in, hgrn_lb_logits, hgrn_onorm, w_out, final_gain):
    b, s, d = x.shape
    w = w_out.shape[1]
    nh = N_HEADS
    assert w == nh * LANES, "heads must be one lane tile wide"
    assert norm_gains.shape[0] == 2, "layer 0 is FoX, layer 1 is HGRN2"
    m = b * s
    x2 = x.reshape(m, d)

    wf = fox_w_in[0]
    wq, wk, wv = (wf[:, i * w:(i + 1) * w].astype(_BF16) for i in range(3))
    w_forget = jnp.pad(wf[:, 3 * w:3 * w + nh], ((0, 0), (0, LANES - nh))).astype(_BF16)
    wg = wf[:, 3 * w + nh:].astype(_BF16)
    q, k, v, g, flog = _proj_call(
        x2, norm_gains[0:1], [wq, wk, wv, wg], w_forget,
        ["scale", "none", "none", "none"], [_BF16] * 4, q_scale=float(LANES) ** -0.5,
        name="fox_in_proj")
    b_pad = jnp.pad(fox_b_f[0:1], ((0, 0), (0, LANES - nh)))
    c = _decay_call(flog.reshape(b, s, LANES), b_pad)
    to3 = lambda t: t.reshape(b, s, w)
    y = _fox_attn_call(to3(q), to3(k), to3(v), to3(g), c)
    x2, h1 = _out_proj_call(y.reshape(m, w), w_out[0].astype(_BF16), x2, norm_gains[1:2],
                            emit_residual=True, norm_dtype=_BF16, name="fox_out_proj")

    wh = hgrn_w_in[0]
    whs = [wh[:, i * w:(i + 1) * w].astype(_BF16) for i in range(4)]
    q, fz, iv, g = _proj_call(h1, None, whs, None, ["silu", "none", "none", "none"],
                              [_BF16, _F32, _BF16, _BF16], name="hgrn_in_proj")
    y = _hgrn_call(to3(q), to3(fz), to3(iv), to3(g), hgrn_lb_logits, hgrn_onorm[0:1], layer=1)
    (out,) = _out_proj_call(y.reshape(m, w), w_out[1].astype(_BF16), x2, final_gain.reshape(1, d),
                            emit_residual=False, norm_dtype=x.dtype, name="hgrn_out_proj")
    return out.reshape(b, s, d)
```

```python
import functools

import numpy as np
import jax
import jax.numpy as jnp
from jax import lax
from jax.experimental import pallas as pl
from jax.experimental.pallas import tpu as pltpu

EPS = 1e-6
LOG2E = 1.4426950408889634
N_HEADS = 16
LANES = 128
HGRN_CHUNK = 128
VMEM_LIMIT = 56 * 1024 * 1024

_F32 = jnp.float32
_BF16 = jnp.bfloat16
_NT = (((1,), (1,)), ((), ()))


def _silu(x):
    return x / (1.0 + jnp.exp(-x))


def _split_bf16(x, terms):
    pieces = []
    rem = x
    for _ in range(terms):
        p = rem.astype(_BF16)
        pieces.append(p)
        rem = rem - p.astype(_F32)
    return pieces


def _params(semantics):
    return pltpu.CompilerParams(dimension_semantics=semantics, vmem_limit_bytes=VMEM_LIMIT)


def _proj_kernel(*refs, apply_norm, n_groups, epilogues, has_extra, q_scale):
    it = iter(refs)
    x_ref = next(it)
    gain_ref = next(it) if apply_norm else None
    w_refs = [next(it) for _ in range(n_groups)]
    wx_ref = next(it) if has_extra else None
    o_refs = [next(it) for _ in range(n_groups)]
    ox_ref = next(it) if has_extra else None
    h_scr = next(it) if apply_norm else None

    j = pl.program_id(1)

    if apply_norm:
        @pl.when(j == 0)
        def _():
            xf = x_ref[...].astype(_F32)
            y = xf * lax.rsqrt(jnp.mean(xf * xf, axis=-1, keepdims=True) + EPS)
            h_scr[...] = (y * gain_ref[...]).astype(_BF16)
            if has_extra:
                ox_ref[...] = jnp.dot(h_scr[...], wx_ref[...], preferred_element_type=_F32)
        h = h_scr[...]
    else:
        h = x_ref[...]
        if has_extra:
            @pl.when(j == 0)
            def _():
                ox_ref[...] = jnp.dot(h, wx_ref[...], preferred_element_type=_F32)

    for g in range(n_groups):
        acc = jnp.dot(h, w_refs[g][...], preferred_element_type=_F32)
        if epilogues[g] == "scale":
            acc = acc * q_scale
        elif epilogues[g] == "silu":
            acc = _silu(acc)
        o_refs[g][...] = acc.astype(o_refs[g].dtype)


def _proj_call(x, gain, weights, extra_w, epilogues, out_dtypes, *, q_scale=1.0, tm=512, tn=512, name):
    m, d = x.shape
    n = weights[0].shape[1]
    tm, tn = min(tm, m), min(tn, n)
    assert m % tm == 0 and n % tn == 0
    apply_norm = gain is not None
    has_extra = extra_w is not None
    n_groups = len(weights)

    in_specs = [pl.BlockSpec((tm, d), lambda i, j: (i, 0))]
    args = [x]
    if apply_norm:
        in_specs.append(pl.BlockSpec((1, d), lambda i, j: (0, 0)))
        args.append(gain)
    for w in weights:
        in_specs.append(pl.BlockSpec((d, tn), lambda i, j: (0, j)))
        args.append(w)
    out_specs = [pl.BlockSpec((tm, tn), lambda i, j: (i, j)) for _ in weights]
    out_shape = [jax.ShapeDtypeStruct((m, n), dt) for dt in out_dtypes]
    if has_extra:
        in_specs.append(pl.BlockSpec((d, LANES), lambda i, j: (0, 0)))
        args.append(extra_w)
        out_specs.append(pl.BlockSpec((tm, LANES), lambda i, j: (i, 0)))
        out_shape.append(jax.ShapeDtypeStruct((m, LANES), _F32))
    scratch = [pltpu.VMEM((tm, d), _BF16)] if apply_norm else []

    kern = functools.partial(_proj_kernel, apply_norm=apply_norm, n_groups=n_groups,
                             epilogues=tuple(epilogues), has_extra=has_extra, q_scale=q_scale)
    return pl.pallas_call(
        kern,
        grid=(m // tm, n // tn),
        in_specs=in_specs,
        out_specs=out_specs,
        out_shape=out_shape,
        scratch_shapes=scratch,
        compiler_params=_params(("parallel", "arbitrary")),
        name=name,
    )(*args)


def _decay_kernel(f_ref, b_ref, c_ref, carry):
    t = pl.program_id(1)

    @pl.when(t == 0)
    def _():
        carry[...] = jnp.zeros_like(carry)

    z = f_ref[0] + b_ref[...]
    lf = jnp.minimum(z, 0.0) - jnp.log(1.0 + jnp.exp(-jnp.abs(z)))
    n = lf.shape[0]
    row = lax.broadcasted_iota(jnp.int32, (n, n), 0)
    col = lax.broadcasted_iota(jnp.int32, (n, n), 1)
    tri = (col <= row).astype(_BF16)
    cs = carry[0:1, :]
    for piece in _split_bf16(lf, 3):
        cs = cs + jnp.dot(tri, piece, preferred_element_type=_F32)
    c_ref[0] = cs
    carry[0:1, :] = cs[n - 1:n, :]


def _decay_call(flog, b_pad, *, tb=512):
    b, s, _ = flog.shape
    tb = min(tb, s)
    assert s % tb == 0
    return pl.pallas_call(
        _decay_kernel,
        grid=(b, s // tb),
        in_specs=[pl.BlockSpec((1, tb, LANES), lambda bi, ti: (bi, ti, 0)),
                  pl.BlockSpec((1, LANES), lambda bi, ti: (0, 0))],
        out_specs=pl.BlockSpec((1, tb, LANES), lambda bi, ti: (bi, ti, 0)),
        out_shape=jax.ShapeDtypeStruct((b, s, LANES), _F32),
        scratch_shapes=[pltpu.VMEM((8, LANES), _F32)],
        compiler_params=_params(("parallel", "arbitrary")),
        name="fox_decay_cumsum",
    )(flog, b_pad)


def _fox_attn_kernel(q_ref, k_ref, v_ref, g_ref, cq_ref, ck_ref, o_ref,
                     kaug, vaug, qaug, m_scr, acc_scr, s_buf, p_buf, a_buf,
                     *, tq, tk, rows_per_group, kb_rows):
    h = pl.program_id(1)
    qi = pl.program_id(2)
    s_len = k_ref.shape[1]
    rg = rows_per_group
    n_groups = tq // rg
    lane = lax.broadcasted_iota(jnp.int32, (1, LANES), 1)

    def head_column(c):
        return jnp.sum(jnp.where(lane == h, c, 0.0), axis=1, keepdims=True) * LOG2E

    @pl.when(qi == 0)
    def _():
        ones_col = jnp.where(lane == 0, 1.0, 0.0).astype(_BF16)

        def build(r, carry):
            rows = pl.ds(pl.multiple_of(r * kb_rows, kb_rows), kb_rows)
            hi, mid, lo = [p.astype(_F32) for p in _split_bf16(head_column(ck_ref[0, rows, :]), 3)]
            aug = jnp.where(lane == 0, -hi, jnp.where(lane == 1, -mid, jnp.where(
                lane == 2, -lo, jnp.where(lane < 6, 1.0, 0.0))))
            kaug[rows, 0:LANES] = k_ref[0, rows, :]
            kaug[rows, LANES:2 * LANES] = aug.astype(_BF16)
            vaug[rows, 0:LANES] = v_ref[0, rows, :]
            vaug[rows, LANES:2 * LANES] = jnp.broadcast_to(ones_col, (kb_rows, LANES))
            return carry
        lax.fori_loop(0, s_len // kb_rows, build, 0)

    hi, mid, lo = [p.astype(_F32) for p in _split_bf16(head_column(cq_ref[0]), 3)]
    qaug[:, 0:LANES] = q_ref[0]
    qaug[:, LANES:2 * LANES] = jnp.where(lane < 3, 1.0, jnp.where(lane == 3, hi, jnp.where(
        lane == 4, mid, jnp.where(lane == 5, lo, 0.0)))).astype(_BF16)
    m_scr[...] = jnp.full_like(m_scr, -jnp.inf)
    acc_scr[...] = jnp.zeros_like(acc_scr)

    def step(g, key_rows, masked):
        grp = slice(g * rg, (g + 1) * rg)
        s = lax.dot_general(qaug[grp, :], kaug[key_rows, :], _NT, preferred_element_type=_F32)
        if masked:
            r = lax.broadcasted_iota(jnp.int32, s.shape, 0)
            c = lax.broadcasted_iota(jnp.int32, s.shape, 1)
            s = jnp.where(c <= r, s, -jnp.inf)
        m_prev = m_scr[grp, :]
        m_new = jnp.maximum(m_prev, jnp.max(s, axis=1, keepdims=True))
        alpha = jnp.exp2(m_prev - m_new)
        p = jnp.exp2(s - m_new).astype(_BF16)
        acc_scr[grp, :] = alpha * acc_scr[grp, :] + jnp.dot(
            p, vaug[key_rows, :], preferred_element_type=_F32)
        m_scr[grp, :] = m_new

    for g in range(n_groups):
        for t in range(g + 1):
            key_rows = pl.ds(pl.multiple_of(qi * tq + t * rg, rg), rg)
            step(g, key_rows, masked=(t == g))

    blocks_per_tile = tq // tk
    n_full = qi * blocks_per_tile

    def key_block(j):
        return pl.ds(pl.multiple_of(j * tk, tk), tk)

    def scores(j, slot):
        s_buf[slot] = lax.dot_general(qaug[...], kaug[key_block(j), :], _NT,
                                      preferred_element_type=_F32)

    def softmax(slot):
        for g in range(n_groups):
            grp = slice(g * rg, (g + 1) * rg)
            s = s_buf[slot, grp, :]
            m_prev = m_scr[grp, :]
            m_new = jnp.maximum(m_prev, jnp.max(s, axis=1, keepdims=True))
            a_buf[slot, grp, :] = jnp.exp2(m_prev - m_new)
            p_buf[slot, grp, :] = jnp.exp2(s - m_new).astype(_BF16)
            m_scr[grp, :] = m_new

    def values(j, slot):
        acc_scr[...] = a_buf[slot] * acc_scr[...] + jnp.dot(
            p_buf[slot], vaug[key_block(j), :], preferred_element_type=_F32)

    @pl.when(qi > 0)
    def _():
        scores(0, 0)
        p_buf[1] = jnp.zeros(p_buf.shape[1:], p_buf.dtype)
        a_buf[1] = jnp.ones(a_buf.shape[1:], a_buf.dtype)

        def pair(i, carry):
            j0 = 2 * i
            values(jnp.maximum(j0 - 1, 0), 1)
            softmax(0)
            scores(j0 + 1, 1)
            values(j0, 0)
            softmax(1)
            scores(jnp.minimum(j0 + 2, n_full - 1), 0)
            return carry
        lax.fori_loop(0, n_full // 2, pair, 0)
        values(n_full - 1, 1)

    acc = acc_scr[...]
    o = acc[:, 0:LANES] / acc[:, LANES:LANES + 1]
    o_ref[0] = (o * _silu(g_ref[0].astype(_F32))).astype(o_ref.dtype)


def _fox_attn_call(q, k, v, g, c, *, tq=1024, tk=512, rows_per_group=256):
    b, s, w = q.shape
    nh = w // LANES
    tq = min(tq, s)
    tk = min(tk, tq // 2)
    rows_per_group = min(rows_per_group, tq)
    assert s % tq == 0 and tq % rows_per_group == 0
    assert tq % (2 * tk) == 0, "the pipelined loop consumes key blocks in pairs"
    kern = functools.partial(_fox_attn_kernel, tq=tq, tk=tk, rows_per_group=rows_per_group,
                             kb_rows=min(512, s))
    head_tile = lambda bi, hi, qi: (bi, qi, hi)
    head_full = lambda bi, hi, qi: (bi, 0, hi)
    return pl.pallas_call(
        kern,
        grid=(b, nh, s // tq),
        in_specs=[pl.BlockSpec((1, tq, LANES), head_tile),
                  pl.BlockSpec((1, s, LANES), head_full),
                  pl.BlockSpec((1, s, LANES), head_full),
                  pl.BlockSpec((1, tq, LANES), head_tile),
                  pl.BlockSpec((1, tq, LANES), lambda bi, hi, qi: (bi, qi, 0)),
                  pl.BlockSpec((1, s, LANES), lambda bi, hi, qi: (bi, 0, 0))],
        out_specs=pl.BlockSpec((1, tq, LANES), head_tile),
        out_shape=jax.ShapeDtypeStruct((b, s, w), _BF16),
        scratch_shapes=[pltpu.VMEM((s, 2 * LANES), _BF16),
                        pltpu.VMEM((s, 2 * LANES), _BF16),
                        pltpu.VMEM((tq, 2 * LANES), _BF16),
                        pltpu.VMEM((tq, 1), _F32),
                        pltpu.VMEM((tq, 2 * LANES), _F32),
                        pltpu.VMEM((2, tq, tk), _F32),
                        pltpu.VMEM((2, tq, tk), _BF16),
                        pltpu.VMEM((2, tq, 1), _F32)],
        compiler_params=_params(("parallel", "parallel", "arbitrary")),
        name="fox_attention",
    )(q, k, v, g, c, c)


def _out_proj_kernel(y_ref, w_ref, x_ref, gain_ref, *o_refs, emit_residual):
    xn = x_ref[...] + jnp.dot(y_ref[...], w_ref[...], preferred_element_type=_F32)
    normed = xn * lax.rsqrt(jnp.mean(xn * xn, axis=-1, keepdims=True) + EPS) * gain_ref[...]
    if emit_residual:
        o_refs[0][...] = xn
        o_refs[1][...] = normed.astype(o_refs[1].dtype)
    else:
        o_refs[0][...] = normed.astype(o_refs[0].dtype)


def _out_proj_call(y, w, x, gain, *, emit_residual, norm_dtype, tm=256, name):
    m, wd = y.shape
    d = w.shape[1]
    tm = min(tm, m)
    assert m % tm == 0
    row_tile = lambda i: (i, 0)
    out_specs = [pl.BlockSpec((tm, d), row_tile)]
    out_shape = [jax.ShapeDtypeStruct((m, d), norm_dtype)]
    if emit_residual:
        out_specs.insert(0, pl.BlockSpec((tm, d), row_tile))
        out_shape.insert(0, jax.ShapeDtypeStruct((m, d), _F32))
    return pl.pallas_call(
        functools.partial(_out_proj_kernel, emit_residual=emit_residual),
        grid=(m // tm,),
        in_specs=[pl.BlockSpec((tm, wd), row_tile),
                  pl.BlockSpec((wd, d), lambda i: (0, 0)),
                  pl.BlockSpec((tm, d), row_tile),
                  pl.BlockSpec((1, d), lambda i: (0, 0))],
        out_specs=out_specs,
        out_shape=out_shape,
        compiler_params=_params(("parallel",)),
        name=name,
    )(y, w, x, gain)


def _hgrn_constants(c):
    t = np.arange(c)[:, None]
    r = np.arange(c)[None, :]
    sums, masks = [], [np.eye(c, dtype=bool)]
    m = 1
    while m < c:
        same = (t // m) == (r // m)
        upper = ((t // m) % 2) == 1
        sums.append(same & np.where(upper, r <= t, r > t))
        masks.append(upper & ((r // m) == (t // m) - 1))
        m *= 2
    sums.append(r <= t)
    return (np.concatenate(sums, 0).astype(np.float32), np.stack(masks).astype(np.float32))


def _hgrn_kernel(q_ref, fz_ref, v_ref, g_ref, lbl_ref, on_ref, sums_ref, masks_ref, o_ref,
                 state, *, layer, chunk):
    si = pl.program_id(2)
    n_levels = masks_ref.shape[0]
    tb = q_ref.shape[1]

    @pl.when(si == 0)
    def _():
        state[...] = jnp.zeros_like(state)

    logits = lbl_ref[...].astype(_F32)
    e = jnp.exp(logits - jnp.max(logits, axis=0, keepdims=True))
    sm = e / jnp.sum(e, axis=0, keepdims=True)
    lb = jnp.sum(sm[1:layer + 1, :], axis=0, keepdims=True) if layer >= 1 else jnp.zeros_like(sm[0:1, :])
    gain = on_ref[...]

    def chunk_step(ci, carry):
        rows = pl.ds(pl.multiple_of(ci * chunk, chunk), chunk)
        q = q_ref[0, rows, :].astype(_F32)
        fz = fz_ref[0, rows, :]
        v = v_ref[0, rows, :]
        ez = jnp.exp(-jnp.abs(fz))
        rz = 1.0 / (1.0 + ez)
        pos = fz >= 0.0
        sig = jnp.where(pos, rz, ez * rz)
        nsig = jnp.where(pos, ez * rz, rz)
        lf = jnp.log(lb + (1.0 - lb) * sig)
        kk = (1.0 - lb) * nsig

        lf_hi, lf_lo = _split_bf16(lf, 2)
        d_all = jnp.dot(sums_ref[...], jnp.concatenate([lf_hi, lf_lo], axis=1),
                        preferred_element_type=_F32)
        d_all = d_all[:, 0:LANES] + d_all[:, LANES:2 * LANES]

        q_b = q.astype(_BF16)
        a = lax.dot_general(q_b, kk.astype(_BF16), _NT, preferred_element_type=_F32) * masks_ref[0]
        for lvl in range(1, n_levels):
            dec = jnp.exp(d_all[(lvl - 1) * chunk:lvl * chunk, :])
            a_l = lax.dot_general((q * dec).astype(_BF16), (kk * dec).astype(_BF16), _NT,
                                  preferred_element_type=_F32)
            a = a + a_l * masks_ref[lvl]

        bcum = d_all[(n_levels - 1) * chunk:n_levels * chunk, :]
        b_last = bcum[chunk - 1:chunk, :]
        st = state[...]
        inter = lax.dot_general((q * jnp.exp(bcum)).astype(_BF16), st.astype(_BF16), _NT,
                                preferred_element_type=_F32)
        o = inter + jnp.dot(a.astype(_BF16), v, preferred_element_type=_F32)

        k_dec = (kk * jnp.exp(b_last - bcum)).astype(_BF16)
        v_t = v.astype(_F32).T.astype(_BF16)
        state[...] = st * jnp.exp(b_last) + jnp.dot(v_t, k_dec, preferred_element_type=_F32)

        o = o * lax.rsqrt(jnp.mean(o * o, axis=-1, keepdims=True) + EPS)
        y = o * gain * _silu(g_ref[0, rows, :].astype(_F32))
        o_ref[0, rows, :] = y.astype(o_ref.dtype)
        return carry

    lax.fori_loop(0, tb // chunk, chunk_step, 0)


def _hgrn_call(q, fz, v, g, lb_logits, onorm, *, layer, tb=1024):
    b, s, w = q.shape
    nh = w // LANES
    tb = min(tb, s)
    chunk = min(HGRN_CHUNK, tb)
    assert s % tb == 0 and tb % chunk == 0
    sums, masks = _hgrn_constants(chunk)
    sums = jnp.asarray(sums, _BF16)
    masks = jnp.asarray(masks, _F32)
    depth = lb_logits.shape[0]
    head_tile = lambda bi, hi, si: (bi, si, hi)
    return pl.pallas_call(
        functools.partial(_hgrn_kernel, layer=layer, chunk=chunk),
        grid=(b, nh, s // tb),
        in_specs=[pl.BlockSpec((1, tb, LANES), head_tile),
                  pl.BlockSpec((1, tb, LANES), head_tile),
                  pl.BlockSpec((1, tb, LANES), head_tile),
                  pl.BlockSpec((1, tb, LANES), head_tile),
                  pl.BlockSpec((depth, LANES), lambda bi, hi, si: (0, hi)),
                  pl.BlockSpec((1, LANES), lambda bi, hi, si: (0, hi)),
                  pl.BlockSpec(sums.shape, lambda bi, hi, si: (0, 0)),
                  pl.BlockSpec(masks.shape, lambda bi, hi, si: (0, 0, 0))],
        out_specs=pl.BlockSpec((1, tb, LANES), head_tile),
        out_shape=jax.ShapeDtypeStruct((b, s, w), _BF16),
        scratch_shapes=[pltpu.VMEM((LANES, LANES), _F32)],
        compiler_params=_params(("parallel", "parallel", "arbitrary")),
        name="hgrn2_recurrence",
    )(q, fz, v, g, lb_logits, onorm, sums, masks)


def kernel(x, norm_gains, fox_w_in, fox_b_f, hgrn_w_in, hgrn_lb_logits, hgrn_onorm, w_out, final_gain):
    b, s, d = x.shape
    w = w_out.shape[1]
    nh = N_HEADS
    assert w == nh * LANES, "heads must be one lane tile wide"
    assert norm_gains.shape[0] == 2, "layer 0 is FoX, layer 1 is HGRN2"
    m = b * s
    x2 = x.reshape(m, d)

    wf = fox_w_in[0]
    wq, wk, wv = (wf[:, i * w:(i + 1) * w].astype(_BF16) for i in range(3))
    w_forget = jnp.pad(wf[:, 3 * w:3 * w + nh], ((0, 0), (0, LANES - nh))).astype(_BF16)
    wg = wf[:, 3 * w + nh:].astype(_BF16)
    q, k, v, g, flog = _proj_call(
        x2, norm_gains[0:1], [wq, wk, wv, wg], w_forget,
        ["scale", "none", "none", "none"], [_BF16] * 4, q_scale=float(LANES) ** -0.5 * LOG2E,
        name="fox_in_proj")
    b_pad = jnp.pad(fox_b_f[0:1], ((0, 0), (0, LANES - nh)))
    c = _decay_call(flog.reshape(b, s, LANES), b_pad)
    to3 = lambda t: t.reshape(b, s, w)
    y = _fox_attn_call(to3(q), to3(k), to3(v), to3(g), c)
    x2, h1 = _out_proj_call(y.reshape(m, w), w_out[0].astype(_BF16), x2, norm_gains[1:2],
                            emit_residual=True, norm_dtype=_BF16, name="fox_out_proj")

    wh = hgrn_w_in[0]
    whs = [wh[:, i * w:(i + 1) * w].astype(_BF16) for i in range(4)]
    q, fz, iv, g = _proj_call(h1, None, whs, None, ["silu", "none", "none", "none"],
                              [_BF16, _F32, _BF16, _BF16], name="hgrn_in_proj")
    y = _hgrn_call(to3(q), to3(fz), to3(iv), to3(g), hgrn_lb_logits, hgrn_onorm[0:1], layer=1)
    (out,) = _out_proj_call(y.reshape(m, w), w_out[1].astype(_BF16), x2, final_gain.reshape(1, d),
                            emit_residual=False, norm_dtype=x.dtype, name="hgrn_out_proj")
    return out.reshape(b, s, d)
```

```python
import functools

import numpy as np
import jax
import jax.numpy as jnp
from jax import lax
from jax.experimental import pallas as pl
from jax.experimental.pallas import tpu as pltpu

EPS = 1e-6
LOG2E = 1.4426950408889634
N_HEADS = 16
LANES = 128
HGRN_CHUNK = 128
VMEM_LIMIT = 56 * 1024 * 1024

_F32 = jnp.float32
_BF16 = jnp.bfloat16
_NT = (((1,), (1,)), ((), ()))


def _silu(x):
    return x / (1.0 + jnp.exp(-x))


def _split_bf16(x, terms):
    pieces = []
    rem = x
    for _ in range(terms):
        p = rem.astype(_BF16)
        pieces.append(p)
        rem = rem - p.astype(_F32)
    return pieces


def _params(semantics):
    return pltpu.CompilerParams(dimension_semantics=semantics, vmem_limit_bytes=VMEM_LIMIT)


def _proj_kernel(*refs, apply_norm, n_groups, epilogues, has_extra, q_scale):
    it = iter(refs)
    x_ref = next(it)
    gain_ref = next(it) if apply_norm else None
    w_refs = [next(it) for _ in range(n_groups)]
    wx_ref = next(it) if has_extra else None
    o_refs = [next(it) for _ in range(n_groups)]
    ox_ref = next(it) if has_extra else None
    h_scr = next(it) if apply_norm else None

    j = pl.program_id(1)

    if apply_norm:
        @pl.when(j == 0)
        def _():
            xf = x_ref[...].astype(_F32)
            y = xf * lax.rsqrt(jnp.mean(xf * xf, axis=-1, keepdims=True) + EPS)
            h_scr[...] = (y * gain_ref[...]).astype(_BF16)
            if has_extra:
                ox_ref[...] = jnp.dot(h_scr[...], wx_ref[...], preferred_element_type=_F32)
        h = h_scr[...]
    else:
        h = x_ref[...]
        if has_extra:
            @pl.when(j == 0)
            def _():
                ox_ref[...] = jnp.dot(h, wx_ref[...], preferred_element_type=_F32)

    for g in range(n_groups):
        acc = jnp.dot(h, w_refs[g][...], preferred_element_type=_F32)
        if epilogues[g] == "scale":
            acc = acc * q_scale
        elif epilogues[g] == "silu":
            acc = _silu(acc)
        o_refs[g][...] = acc.astype(o_refs[g].dtype)


def _proj_call(x, gain, weights, extra_w, epilogues, out_dtypes, *, q_scale=1.0, tm=512, tn=512, name):
    m, d = x.shape
    n = weights[0].shape[1]
    tm, tn = min(tm, m), min(tn, n)
    assert m % tm == 0 and n % tn == 0
    apply_norm = gain is not None
    has_extra = extra_w is not None
    n_groups = len(weights)

    in_specs = [pl.BlockSpec((tm, d), lambda i, j: (i, 0))]
    args = [x]
    if apply_norm:
        in_specs.append(pl.BlockSpec((1, d), lambda i, j: (0, 0)))
        args.append(gain)
    for w in weights:
        in_specs.append(pl.BlockSpec((d, tn), lambda i, j: (0, j)))
        args.append(w)
    out_specs = [pl.BlockSpec((tm, tn), lambda i, j: (i, j)) for _ in weights]
    out_shape = [jax.ShapeDtypeStruct((m, n), dt) for dt in out_dtypes]
    if has_extra:
        in_specs.append(pl.BlockSpec((d, LANES), lambda i, j: (0, 0)))
        args.append(extra_w)
        out_specs.append(pl.BlockSpec((tm, LANES), lambda i, j: (i, 0)))
        out_shape.append(jax.ShapeDtypeStruct((m, LANES), _F32))
    scratch = [pltpu.VMEM((tm, d), _BF16)] if apply_norm else []

    kern = functools.partial(_proj_kernel, apply_norm=apply_norm, n_groups=n_groups,
                             epilogues=tuple(epilogues), has_extra=has_extra, q_scale=q_scale)
    return pl.pallas_call(
        kern,
        grid=(m // tm, n // tn),
        in_specs=in_specs,
        out_specs=out_specs,
        out_shape=out_shape,
        scratch_shapes=scratch,
        compiler_params=_params(("parallel", "arbitrary")),
        name=name,
    )(*args)


def _decay_kernel(f_ref, b_ref, c_ref, carry):
    t = pl.program_id(1)

    @pl.when(t == 0)
    def _():
        carry[...] = jnp.zeros_like(carry)

    z = f_ref[0] + b_ref[...]
    lf = jnp.minimum(z, 0.0) - jnp.log(1.0 + jnp.exp(-jnp.abs(z)))
    n = lf.shape[0]
    row = lax.broadcasted_iota(jnp.int32, (n, n), 0)
    col = lax.broadcasted_iota(jnp.int32, (n, n), 1)
    tri = (col <= row).astype(_BF16)
    cs = carry[0:1, :]
    for piece in _split_bf16(lf, 3):
        cs = cs + jnp.dot(tri, piece, preferred_element_type=_F32)
    c_ref[0] = cs
    carry[0:1, :] = cs[n - 1:n, :]


def _decay_call(flog, b_pad, *, tb=512):
    b, s, _ = flog.shape
    tb = min(tb, s)
    assert s % tb == 0
    return pl.pallas_call(
        _decay_kernel,
        grid=(b, s // tb),
        in_specs=[pl.BlockSpec((1, tb, LANES), lambda bi, ti: (bi, ti, 0)),
                  pl.BlockSpec((1, LANES), lambda bi, ti: (0, 0))],
        out_specs=pl.BlockSpec((1, tb, LANES), lambda bi, ti: (bi, ti, 0)),
        out_shape=jax.ShapeDtypeStruct((b, s, LANES), _F32),
        scratch_shapes=[pltpu.VMEM((8, LANES), _F32)],
        compiler_params=_params(("parallel", "arbitrary")),
        name="fox_decay_cumsum",
    )(flog, b_pad)


def _fox_attn_kernel(q_ref, k_ref, v_ref, g_ref, cq_ref, ck_ref, o_ref,
                     kaug, vaug, qaug, m_scr, acc_scr, s_buf, p_buf, a_buf,
                     *, tq, tk, rows_per_group, kb_rows):
    h = pl.program_id(1)
    qi = pl.program_id(2)
    s_len = k_ref.shape[1]
    rg = rows_per_group
    n_groups = tq // rg
    lane = lax.broadcasted_iota(jnp.int32, (1, LANES), 1)

    def head_column(c):
        return jnp.sum(jnp.where(lane == h, c, 0.0), axis=1, keepdims=True) * LOG2E

    @pl.when(qi == 0)
    def _():
        ones_col = jnp.where(lane == 0, 1.0, 0.0).astype(_BF16)

        def build(r, carry):
            rows = pl.ds(pl.multiple_of(r * kb_rows, kb_rows), kb_rows)
            hi, mid, lo = [p.astype(_F32) for p in _split_bf16(head_column(ck_ref[0, rows, :]), 3)]
            aug = jnp.where(lane == 0, -hi, jnp.where(lane == 1, -mid, jnp.where(
                lane == 2, -lo, jnp.where(lane < 6, 1.0, 0.0))))
            kaug[rows, 0:LANES] = k_ref[0, rows, :]
            kaug[rows, LANES:2 * LANES] = aug.astype(_BF16)
            vaug[rows, 0:LANES] = v_ref[0, rows, :]
            vaug[rows, LANES:2 * LANES] = jnp.broadcast_to(ones_col, (kb_rows, LANES))
            return carry
        lax.fori_loop(0, s_len // kb_rows, build, 0)

    hi, mid, lo = [p.astype(_F32) for p in _split_bf16(head_column(cq_ref[0]), 3)]
    qaug[:, 0:LANES] = q_ref[0]
    qaug[:, LANES:2 * LANES] = jnp.where(lane < 3, 1.0, jnp.where(lane == 3, hi, jnp.where(
        lane == 4, mid, jnp.where(lane == 5, lo, 0.0)))).astype(_BF16)
    m_scr[...] = jnp.full_like(m_scr, -jnp.inf)
    acc_scr[...] = jnp.zeros_like(acc_scr)

    n_full = 2 * qi
    all_rows = slice(0, tq)
    late_rows = slice(tk, tq)

    def key_block(j):
        return pl.ds(pl.multiple_of(j * tk, tk), tk)

    def scores(j, slot, rows=all_rows):
        s_buf[slot, rows, :] = lax.dot_general(qaug[rows, :], kaug[key_block(j), :], _NT,
                                               preferred_element_type=_F32)

    def softmax(slot, diag=None):
        for g in range(n_groups):
            row0 = g * rg
            key0 = 0 if diag is None else diag * tk
            if diag is not None and row0 + rg <= key0:
                continue
            grp = slice(row0, row0 + rg)
            s = s_buf[slot, grp, :]
            if diag is not None and row0 < key0 + tk - 1:
                r = lax.broadcasted_iota(jnp.int32, s.shape, 0) + row0
                c = lax.broadcasted_iota(jnp.int32, s.shape, 1) + key0
                s = jnp.where(c <= r, s, -jnp.inf)
            m_prev = m_scr[grp, :]
            m_new = jnp.maximum(m_prev, jnp.max(s, axis=1, keepdims=True))
            a_buf[slot, grp, :] = jnp.exp2(m_prev - m_new)
            p_buf[slot, grp, :] = jnp.exp2(s - m_new).astype(_BF16)
            m_scr[grp, :] = m_new

    def values(j, slot, rows=all_rows):
        acc_scr[rows, :] = a_buf[slot, rows, :] * acc_scr[rows, :] + jnp.dot(
            p_buf[slot, rows, :], vaug[key_block(j), :], preferred_element_type=_F32)

    scores(0, 0)
    p_buf[1] = jnp.zeros(p_buf.shape[1:], p_buf.dtype)
    a_buf[1] = jnp.ones(a_buf.shape[1:], a_buf.dtype)

    def pair(i, carry):
        j0 = 2 * i
        values(jnp.maximum(j0 - 1, 0), 1)
        softmax(0)
        scores(j0 + 1, 1)
        values(j0, 0)
        softmax(1)
        scores(j0 + 2, 0)
        return carry
    lax.fori_loop(0, qi, pair, 0)

    values(jnp.maximum(n_full - 1, 0), 1)
    softmax(0, diag=0)
    scores(n_full + 1, 1, late_rows)
    values(n_full, 0)
    softmax(1, diag=1)
    values(n_full + 1, 1, late_rows)

    acc = acc_scr[...]
    o = acc[:, 0:LANES] / acc[:, LANES:LANES + 1]
    o_ref[0] = (o * _silu(g_ref[0].astype(_F32))).astype(o_ref.dtype)


def _fox_attn_call(q, k, v, g, c, *, tq=1024, tk=512, rows_per_group=256):
    b, s, w = q.shape
    nh = w // LANES
    tq = min(tq, s)
    tk = min(tk, tq // 2)
    rows_per_group = min(rows_per_group, tq)
    assert s % tq == 0 and tq % rows_per_group == 0
    assert tq == 2 * tk, "the diagonal tile is handled as exactly two key blocks"
    kern = functools.partial(_fox_attn_kernel, tq=tq, tk=tk, rows_per_group=rows_per_group,
                             kb_rows=min(512, s))
    head_tile = lambda bi, hi, qi: (bi, qi, hi)
    head_full = lambda bi, hi, qi: (bi, 0, hi)
    return pl.pallas_call(
        kern,
        grid=(b, nh, s // tq),
        in_specs=[pl.BlockSpec((1, tq, LANES), head_tile),
                  pl.BlockSpec((1, s, LANES), head_full),
                  pl.BlockSpec((1, s, LANES), head_full),
                  pl.BlockSpec((1, tq, LANES), head_tile),
                  pl.BlockSpec((1, tq, LANES), lambda bi, hi, qi: (bi, qi, 0)),
                  pl.BlockSpec((1, s, LANES), lambda bi, hi, qi: (bi, 0, 0))],
        out_specs=pl.BlockSpec((1, tq, LANES), head_tile),
        out_shape=jax.ShapeDtypeStruct((b, s, w), _BF16),
        scratch_shapes=[pltpu.VMEM((s, 2 * LANES), _BF16),
                        pltpu.VMEM((s, 2 * LANES), _BF16),
                        pltpu.VMEM((tq, 2 * LANES), _BF16),
                        pltpu.VMEM((tq, 1), _F32),
                        pltpu.VMEM((tq, 2 * LANES), _F32),
                        pltpu.VMEM((2, tq, tk), _F32),
                        pltpu.VMEM((2, tq, tk), _BF16),
                        pltpu.VMEM((2, tq, 1), _F32)],
        compiler_params=_params(("parallel", "parallel", "arbitrary")),
        name="fox_attention",
    )(q, k, v, g, c, c)


def _out_proj_kernel(y_ref, w_ref, x_ref, gain_ref, *o_refs, emit_residual):
    xn = x_ref[...] + jnp.dot(y_ref[...], w_ref[...], preferred_element_type=_F32)
    normed = xn * lax.rsqrt(jnp.mean(xn * xn, axis=-1, keepdims=True) + EPS) * gain_ref[...]
    if emit_residual:
        o_refs[0][...] = xn
        o_refs[1][...] = normed.astype(o_refs[1].dtype)
    else:
        o_refs[0][...] = normed.astype(o_refs[0].dtype)


def _out_proj_call(y, w, x, gain, *, emit_residual, norm_dtype, tm=256, name):
    m, wd = y.shape
    d = w.shape[1]
    tm = min(tm, m)
    assert m % tm == 0
    row_tile = lambda i: (i, 0)
    out_specs = [pl.BlockSpec((tm, d), row_tile)]
    out_shape = [jax.ShapeDtypeStruct((m, d), norm_dtype)]
    if emit_residual:
        out_specs.insert(0, pl.BlockSpec((tm, d), row_tile))
        out_shape.insert(0, jax.ShapeDtypeStruct((m, d), _F32))
    return pl.pallas_call(
        functools.partial(_out_proj_kernel, emit_residual=emit_residual),
        grid=(m // tm,),
        in_specs=[pl.BlockSpec((tm, wd), row_tile),
                  pl.BlockSpec((wd, d), lambda i: (0, 0)),
                  pl.BlockSpec((tm, d), row_tile),
                  pl.BlockSpec((1, d), lambda i: (0, 0))],
        out_specs=out_specs,
        out_shape=out_shape,
        compiler_params=_params(("parallel",)),
        name=name,
    )(y, w, x, gain)


def _hgrn_constants(c):
    t = np.arange(c)[:, None]
    r = np.arange(c)[None, :]
    sums, masks = [], [np.eye(c, dtype=bool)]
    m = 1
    while m < c:
        same = (t // m) == (r // m)
        upper = ((t // m) % 2) == 1
        sums.append(same & np.where(upper, r <= t, r > t))
        masks.append(upper & ((r // m) == (t // m) - 1))
        m *= 2
    sums.append(r <= t)
    return (np.concatenate(sums, 0).astype(np.float32), np.stack(masks).astype(np.float32))


def _hgrn_kernel(q_ref, fz_ref, v_ref, g_ref, lbl_ref, on_ref, sums_ref, masks_ref, o_ref,
                 state, *, layer, chunk, heads):
    si = pl.program_id(2)
    n_levels = masks_ref.shape[0]
    tb = q_ref.shape[1]

    @pl.when(si == 0)
    def _():
        state[...] = jnp.zeros_like(state)

    logits = lbl_ref[...].astype(_F32)
    e = jnp.exp(logits - jnp.max(logits, axis=0, keepdims=True))
    sm = e / jnp.sum(e, axis=0, keepdims=True)
    lb_all = jnp.sum(sm[1:layer + 1, :], axis=0, keepdims=True) if layer >= 1 else jnp.zeros_like(sm[0:1, :])
    gain_all = on_ref[...]

    def head_chunk(hh, rows):
        cols = slice(hh * LANES, (hh + 1) * LANES)
        lb = lb_all[:, cols]
        gain = gain_all[:, cols]
        q = q_ref[0, rows, cols].astype(_F32)
        fz = fz_ref[0, rows, cols]
        v = v_ref[0, rows, cols]
        ez = jnp.exp(-jnp.abs(fz))
        rz = 1.0 / (1.0 + ez)
        pos = fz >= 0.0
        sig = jnp.where(pos, rz, ez * rz)
        nsig = jnp.where(pos, ez * rz, rz)
        lf = jnp.log(lb + (1.0 - lb) * sig)
        kk = (1.0 - lb) * nsig

        lf_hi, lf_lo = _split_bf16(lf, 2)
        d_all = jnp.dot(sums_ref[...], jnp.concatenate([lf_hi, lf_lo], axis=1),
                        preferred_element_type=_F32)
        d_all = d_all[:, 0:LANES] + d_all[:, LANES:2 * LANES]

        q_b = q.astype(_BF16)
        a = lax.dot_general(q_b, kk.astype(_BF16), _NT, preferred_element_type=_F32) * masks_ref[0]
        for lvl in range(1, n_levels):
            dec = jnp.exp(d_all[(lvl - 1) * chunk:lvl * chunk, :])
            a_l = lax.dot_general((q * dec).astype(_BF16), (kk * dec).astype(_BF16), _NT,
                                  preferred_element_type=_F32)
            a = a + a_l * masks_ref[lvl]

        bcum = d_all[(n_levels - 1) * chunk:n_levels * chunk, :]
        b_last = bcum[chunk - 1:chunk, :]
        st = state[hh]
        inter = lax.dot_general((q * jnp.exp(bcum)).astype(_BF16), st.astype(_BF16), _NT,
                                preferred_element_type=_F32)
        o = inter + jnp.dot(a.astype(_BF16), v, preferred_element_type=_F32)

        k_dec = (kk * jnp.exp(b_last - bcum)).astype(_BF16)
        v_t = v.astype(_F32).T.astype(_BF16)
        state[hh] = st * jnp.exp(b_last) + jnp.dot(v_t, k_dec, preferred_element_type=_F32)

        o = o * lax.rsqrt(jnp.mean(o * o, axis=-1, keepdims=True) + EPS)
        y = o * gain * _silu(g_ref[0, rows, cols].astype(_F32))
        o_ref[0, rows, cols] = y.astype(o_ref.dtype)

    def chunk_step(ci, carry):
        rows = pl.ds(pl.multiple_of(ci * chunk, chunk), chunk)
        for hh in range(heads):
            head_chunk(hh, rows)
        return carry

    lax.fori_loop(0, tb // chunk, chunk_step, 0)


def _hgrn_call(q, fz, v, g, lb_logits, onorm, *, layer, tb=1024, heads=2):
    b, s, w = q.shape
    nh = w // LANES
    tb = min(tb, s)
    chunk = min(HGRN_CHUNK, tb)
    assert s % tb == 0 and tb % chunk == 0 and nh % heads == 0
    sums, masks = _hgrn_constants(chunk)
    sums = jnp.asarray(sums, _BF16)
    masks = jnp.asarray(masks, _F32)
    depth = lb_logits.shape[0]
    hw = heads * LANES
    head_tile = lambda bi, hi, si: (bi, si, hi)
    return pl.pallas_call(
        functools.partial(_hgrn_kernel, layer=layer, chunk=chunk, heads=heads),
        grid=(b, nh // heads, s // tb),
        in_specs=[pl.BlockSpec((1, tb, hw), head_tile),
                  pl.BlockSpec((1, tb, hw), head_tile),
                  pl.BlockSpec((1, tb, hw), head_tile),
                  pl.BlockSpec((1, tb, hw), head_tile),
                  pl.BlockSpec((depth, hw), lambda bi, hi, si: (0, hi)),
                  pl.BlockSpec((1, hw), lambda bi, hi, si: (0, hi)),
                  pl.BlockSpec(sums.shape, lambda bi, hi, si: (0, 0)),
                  pl.BlockSpec(masks.shape, lambda bi, hi, si: (0, 0, 0))],
        out_specs=pl.BlockSpec((1, tb, hw), head_tile),
        out_shape=jax.ShapeDtypeStruct((b, s, w), _BF16),
        scratch_shapes=[pltpu.VMEM((heads, LANES, LANES), _F32)],
        compiler_params=_params(("parallel", "parallel", "arbitrary")),
        name="hgrn2_recurrence",
    )(q, fz, v, g, lb_logits, onorm, sums, masks)


def kernel(x, norm_gains, fox_w_in, fox_b_f, hgrn_w_in, hgrn_lb_logits, hgrn_onorm, w_out, final_gain):
    b, s, d = x.shape
    w = w_out.shape[1]
    nh = N_HEADS
    assert w == nh * LANES, "heads must be one lane tile wide"
    assert norm_gains.shape[0] == 2, "layer 0 is FoX, layer 1 is HGRN2"
    m = b * s
    x2 = x.reshape(m, d)

    wf = fox_w_in[0]
    wq, wk, wv = (wf[:, i * w:(i + 1) * w].astype(_BF16) for i in range(3))
    w_forget = jnp.pad(wf[:, 3 * w:3 * w + nh], ((0, 0), (0, LANES - nh))).astype(_BF16)
    wg = wf[:, 3 * w + nh:].astype(_BF16)
    q, k, v, g, flog = _proj_call(
        x2, norm_gains[0:1], [wq, wk, wv, wg], w_forget,
        ["scale", "none", "none", "none"], [_BF16] * 4, q_scale=float(LANES) ** -0.5 * LOG2E,
        name="fox_in_proj")
    b_pad = jnp.pad(fox_b_f[0:1], ((0, 0), (0, LANES - nh)))
    c = _decay_call(flog.reshape(b, s, LANES), b_pad)
    to3 = lambda t: t.reshape(b, s, w)
    y = _fox_attn_call(to3(q), to3(k), to3(v), to3(g), c)
    x2, h1 = _out_proj_call(y.reshape(m, w), w_out[0].astype(_BF16), x2, norm_gains[1:2],
                            emit_residual=True, norm_dtype=_BF16, name="fox_out_proj")

    wh = hgrn_w_in[0]
    whs = [wh[:, i * w:(i + 1) * w].astype(_BF16) for i in range(4)]
    q, fz, iv, g = _proj_call(h1, None, whs, None, ["silu", "none", "none", "none"],
                              [_BF16, _F32, _BF16, _BF16], name="hgrn_in_proj")
    y = _hgrn_call(to3(q), to3(fz), to3(iv), to3(g), hgrn_lb_logits, hgrn_onorm[0:1], layer=1)
    (out,) = _out_proj_call(y.reshape(m, w), w_out[1].astype(_BF16), x2, final_gain.reshape(1, d),
                            emit_residual=False, norm_dtype=x.dtype, name="hgrn_out_proj")
    return out.reshape(b, s, d)
```

```python
import functools

import numpy as np
import jax
import jax.numpy as jnp
from jax import lax
from jax.experimental import pallas as pl
from jax.experimental.pallas import tpu as pltpu

EPS = 1e-6
LOG2E = 1.4426950408889634
N_HEADS = 16
LANES = 128
SUBLANES = 8
HGRN_CHUNK = 128
VMEM_LIMIT = 56 * 1024 * 1024

_F32 = jnp.float32
_BF16 = jnp.bfloat16
_NT = (((1,), (1,)), ((), ()))


def _silu(x):
    return x / (1.0 + jnp.exp(-x))


def _split_bf16(x, terms):
    pieces = []
    rem = x
    for _ in range(terms):
        p = rem.astype(_BF16)
        pieces.append(p)
        rem = rem - p.astype(_F32)
    return pieces


def _params(semantics):
    return pltpu.CompilerParams(dimension_semantics=semantics, vmem_limit_bytes=VMEM_LIMIT)


def _proj_kernel(*refs, apply_norm, n_groups, epilogues, has_extra, q_scale):
    it = iter(refs)
    x_ref = next(it)
    gain_ref = next(it) if apply_norm else None
    w_refs = [next(it) for _ in range(n_groups)]
    wx_ref = next(it) if has_extra else None
    o_refs = [next(it) for _ in range(n_groups)]
    ox_ref = next(it) if has_extra else None
    h_scr = next(it) if apply_norm else None

    j = pl.program_id(1)

    if apply_norm:
        @pl.when(j == 0)
        def _():
            xf = x_ref[...].astype(_F32)
            y = xf * lax.rsqrt(jnp.mean(xf * xf, axis=-1, keepdims=True) + EPS)
            h_scr[...] = (y * gain_ref[...]).astype(_BF16)
            if has_extra:
                ox_ref[...] = jnp.dot(h_scr[...], wx_ref[...], preferred_element_type=_F32)
        h = h_scr[...]
    else:
        h = x_ref[...]
        if has_extra:
            @pl.when(j == 0)
            def _():
                ox_ref[...] = jnp.dot(h, wx_ref[...], preferred_element_type=_F32)

    for g in range(n_groups):
        acc = jnp.dot(h, w_refs[g][...], preferred_element_type=_F32)
        if epilogues[g] == "scale":
            acc = acc * q_scale
        elif epilogues[g] == "silu":
            acc = _silu(acc)
        o_refs[g][...] = acc.astype(o_refs[g].dtype)


def _proj_call(x, gain, weights, extra_w, epilogues, out_dtypes, *, q_scale=1.0, tm=512, tn=512, name):
    m, d = x.shape
    n = weights[0].shape[1]
    tm, tn = min(tm, m), min(tn, n)
    assert m % tm == 0 and n % tn == 0
    apply_norm = gain is not None
    has_extra = extra_w is not None
    n_groups = len(weights)

    in_specs = [pl.BlockSpec((tm, d), lambda i, j: (i, 0))]
    args = [x]
    if apply_norm:
        in_specs.append(pl.BlockSpec((1, d), lambda i, j: (0, 0)))
        args.append(gain)
    for w in weights:
        in_specs.append(pl.BlockSpec((d, tn), lambda i, j: (0, j)))
        args.append(w)
    out_specs = [pl.BlockSpec((tm, tn), lambda i, j: (i, j)) for _ in weights]
    out_shape = [jax.ShapeDtypeStruct((m, n), dt) for dt in out_dtypes]
    if has_extra:
        in_specs.append(pl.BlockSpec((d, LANES), lambda i, j: (0, 0)))
        args.append(extra_w)
        out_specs.append(pl.BlockSpec((tm, LANES), lambda i, j: (i, 0)))
        out_shape.append(jax.ShapeDtypeStruct((m, LANES), _F32))
    scratch = [pltpu.VMEM((tm, d), _BF16)] if apply_norm else []

    kern = functools.partial(_proj_kernel, apply_norm=apply_norm, n_groups=n_groups,
                             epilogues=tuple(epilogues), has_extra=has_extra, q_scale=q_scale)
    return pl.pallas_call(
        kern,
        grid=(m // tm, n // tn),
        in_specs=in_specs,
        out_specs=out_specs,
        out_shape=out_shape,
        scratch_shapes=scratch,
        compiler_params=_params(("parallel", "arbitrary")),
        name=name,
    )(*args)


def _decay_kernel(f_ref, b_ref, c_ref, carry):
    t = pl.program_id(1)

    @pl.when(t == 0)
    def _():
        carry[...] = jnp.zeros_like(carry)

    z = f_ref[0] + b_ref[...]
    lf = jnp.minimum(z, 0.0) - jnp.log(1.0 + jnp.exp(-jnp.abs(z)))
    n = lf.shape[0]
    row = lax.broadcasted_iota(jnp.int32, (n, n), 0)
    col = lax.broadcasted_iota(jnp.int32, (n, n), 1)
    tri = (col <= row).astype(_BF16)
    cs = carry[0:1, :]
    for piece in _split_bf16(lf, 3):
        cs = cs + jnp.dot(tri, piece, preferred_element_type=_F32)
    c_ref[0] = cs
    carry[0:1, :] = cs[n - 1:n, :]


def _decay_call(flog, b_pad, *, tb=512):
    b, s, _ = flog.shape
    tb = min(tb, s)
    assert s % tb == 0
    return pl.pallas_call(
        _decay_kernel,
        grid=(b, s // tb),
        in_specs=[pl.BlockSpec((1, tb, LANES), lambda bi, ti: (bi, ti, 0)),
                  pl.BlockSpec((1, LANES), lambda bi, ti: (0, 0))],
        out_specs=pl.BlockSpec((1, tb, LANES), lambda bi, ti: (bi, ti, 0)),
        out_shape=jax.ShapeDtypeStruct((b, s, LANES), _F32),
        scratch_shapes=[pltpu.VMEM((8, LANES), _F32)],
        compiler_params=_params(("parallel", "arbitrary")),
        name="fox_decay_cumsum",
    )(flog, b_pad)


def _fox_attn_kernel(q_ref, k_ref, v_ref, g_ref, cq_ref, ck_ref, o_ref,
                     kaug, vaug, qaug, m_scr, acc_scr, s_buf, p_buf, a_buf,
                     *, tq, tk, rows_per_group, kb_rows):
    h = pl.program_id(1)
    qi = pl.program_id(2)
    s_len = k_ref.shape[1]
    rg = rows_per_group
    n_groups = tq // rg
    lane = lax.broadcasted_iota(jnp.int32, (1, LANES), 1)

    def head_column(c):
        return jnp.sum(jnp.where(lane == h, c, 0.0), axis=1, keepdims=True) * LOG2E

    @pl.when(qi == 0)
    def _():
        ones_col = jnp.where(lane == 0, 1.0, 0.0).astype(_BF16)

        def build(r, carry):
            rows = pl.ds(pl.multiple_of(r * kb_rows, kb_rows), kb_rows)
            hi, mid, lo = [p.astype(_F32) for p in _split_bf16(head_column(ck_ref[0, rows, :]), 3)]
            aug = jnp.where(lane == 0, -hi, jnp.where(lane == 1, -mid, jnp.where(
                lane == 2, -lo, jnp.where(lane < 6, 1.0, 0.0))))
            kaug[rows, 0:LANES] = k_ref[0, rows, :]
            kaug[rows, LANES:2 * LANES] = aug.astype(_BF16)
            vaug[rows, 0:LANES] = v_ref[0, rows, :]
            vaug[rows, LANES:2 * LANES] = jnp.broadcast_to(ones_col, (kb_rows, LANES))
            return carry
        lax.fori_loop(0, s_len // kb_rows, build, 0)

    hi, mid, lo = [p.astype(_F32) for p in _split_bf16(head_column(cq_ref[0]), 3)]
    qaug[:, 0:LANES] = q_ref[0]
    qaug[:, LANES:2 * LANES] = jnp.where(lane < 3, 1.0, jnp.where(lane == 3, hi, jnp.where(
        lane == 4, mid, jnp.where(lane == 5, lo, 0.0)))).astype(_BF16)
    m_scr[...] = jnp.full_like(m_scr, -jnp.inf)
    acc_scr[...] = jnp.zeros_like(acc_scr)

    n_full = 2 * qi
    all_rows = slice(0, tq)
    late_rows = slice(tk, tq)

    def key_block(j):
        return pl.ds(pl.multiple_of(j * tk, tk), tk)

    def scores(j, slot, rows=all_rows):
        s_buf[slot, rows, :] = lax.dot_general(qaug[rows, :], kaug[key_block(j), :], _NT,
                                               preferred_element_type=_F32)

    def softmax(slot, diag=None):
        for g in range(n_groups):
            row0 = g * rg
            key0 = 0 if diag is None else diag * tk
            if diag is not None and row0 + rg <= key0:
                continue
            grp = slice(row0, row0 + rg)
            s = s_buf[slot, grp, :]
            if diag is not None and row0 < key0 + tk - 1:
                r = lax.broadcasted_iota(jnp.int32, s.shape, 0) + row0
                c = lax.broadcasted_iota(jnp.int32, s.shape, 1) + key0
                s = jnp.where(c <= r, s, -jnp.inf)
            m_prev = m_scr[grp, :]
            m_new = jnp.maximum(m_prev, jnp.max(s, axis=1, keepdims=True))
            a_buf[slot, grp, :] = jnp.exp2(m_prev - m_new)
            p_buf[slot, grp, :] = jnp.exp2(s - m_new).astype(_BF16)
            m_scr[grp, :] = m_new

    def values(j, slot, rows=all_rows):
        acc_scr[rows, :] = a_buf[slot, rows, :] * acc_scr[rows, :] + jnp.dot(
            p_buf[slot, rows, :], vaug[key_block(j), :], preferred_element_type=_F32)

    scores(0, 0)
    p_buf[1] = jnp.zeros(p_buf.shape[1:], p_buf.dtype)
    a_buf[1] = jnp.ones(a_buf.shape[1:], a_buf.dtype)

    def pair(i, carry):
        j0 = 2 * i
        values(jnp.maximum(j0 - 1, 0), 1)
        softmax(0)
        scores(j0 + 1, 1)
        values(j0, 0)
        softmax(1)
        scores(j0 + 2, 0)
        return carry
    lax.fori_loop(0, qi, pair, 0)

    values(jnp.maximum(n_full - 1, 0), 1)
    softmax(0, diag=0)
    scores(n_full + 1, 1, late_rows)
    values(n_full, 0)
    softmax(1, diag=1)
    values(n_full + 1, 1, late_rows)

    acc = acc_scr[...]
    o = acc[:, 0:LANES] / acc[:, LANES:LANES + 1]
    o_ref[0] = (o * _silu(g_ref[0].astype(_F32))).astype(o_ref.dtype)


def _fox_attn_call(q, k, v, g, c, *, tq=1024, tk=512, rows_per_group=256):
    b, s, w = q.shape
    nh = w // LANES
    tq = min(tq, s)
    tk = min(tk, tq // 2)
    rows_per_group = min(rows_per_group, tq)
    assert s % tq == 0 and tq % rows_per_group == 0
    assert tq == 2 * tk, "the diagonal tile is handled as exactly two key blocks"
    kern = functools.partial(_fox_attn_kernel, tq=tq, tk=tk, rows_per_group=rows_per_group,
                             kb_rows=min(512, s))
    head_tile = lambda bi, hi, qi: (bi, qi, hi)
    head_full = lambda bi, hi, qi: (bi, 0, hi)
    return pl.pallas_call(
        kern,
        grid=(b, nh, s // tq),
        in_specs=[pl.BlockSpec((1, tq, LANES), head_tile),
                  pl.BlockSpec((1, s, LANES), head_full),
                  pl.BlockSpec((1, s, LANES), head_full),
                  pl.BlockSpec((1, tq, LANES), head_tile),
                  pl.BlockSpec((1, tq, LANES), lambda bi, hi, qi: (bi, qi, 0)),
                  pl.BlockSpec((1, s, LANES), lambda bi, hi, qi: (bi, 0, 0))],
        out_specs=pl.BlockSpec((1, tq, LANES), head_tile),
        out_shape=jax.ShapeDtypeStruct((b, s, w), _BF16),
        scratch_shapes=[pltpu.VMEM((s, 2 * LANES), _BF16),
                        pltpu.VMEM((s, 2 * LANES), _BF16),
                        pltpu.VMEM((tq, 2 * LANES), _BF16),
                        pltpu.VMEM((tq, 1), _F32),
                        pltpu.VMEM((tq, 2 * LANES), _F32),
                        pltpu.VMEM((2, tq, tk), _F32),
                        pltpu.VMEM((2, tq, tk), _BF16),
                        pltpu.VMEM((2, tq, 1), _F32)],
        compiler_params=_params(("parallel", "parallel", "arbitrary")),
        name="fox_attention",
    )(q, k, v, g, c, c)


def _out_proj_kernel(y_ref, w_ref, x_ref, gain_ref, *o_refs, emit_residual):
    xn = x_ref[...] + jnp.dot(y_ref[...], w_ref[...], preferred_element_type=_F32)
    normed = xn * lax.rsqrt(jnp.mean(xn * xn, axis=-1, keepdims=True) + EPS) * gain_ref[...]
    if emit_residual:
        o_refs[0][...] = xn
        o_refs[1][...] = normed.astype(o_refs[1].dtype)
    else:
        o_refs[0][...] = normed.astype(o_refs[0].dtype)


def _out_proj_call(y, w, x, gain, *, emit_residual, norm_dtype, tm=256, name):
    m, wd = y.shape
    d = w.shape[1]
    tm = min(tm, m)
    assert m % tm == 0
    row_tile = lambda i: (i, 0)
    out_specs = [pl.BlockSpec((tm, d), row_tile)]
    out_shape = [jax.ShapeDtypeStruct((m, d), norm_dtype)]
    if emit_residual:
        out_specs.insert(0, pl.BlockSpec((tm, d), row_tile))
        out_shape.insert(0, jax.ShapeDtypeStruct((m, d), _F32))
    return pl.pallas_call(
        functools.partial(_out_proj_kernel, emit_residual=emit_residual),
        grid=(m // tm,),
        in_specs=[pl.BlockSpec((tm, wd), row_tile),
                  pl.BlockSpec((wd, d), lambda i: (0, 0)),
                  pl.BlockSpec((tm, d), row_tile),
                  pl.BlockSpec((1, d), lambda i: (0, 0))],
        out_specs=out_specs,
        out_shape=out_shape,
        compiler_params=_params(("parallel",)),
        name=name,
    )(y, w, x, gain)


def _hgrn_constants(c):
    t = np.arange(c)[:, None]
    r = np.arange(c)[None, :]
    sums, signs, masks = [], [], [np.eye(c, dtype=bool)]
    m = 1
    while m < c:
        same = (t // m) == (r // m)
        upper = ((t // m) % 2) == 1
        if 2 * m < SUBLANES:
            sums.append(same & np.where(upper, r <= t, r > t))
        else:
            signs.append(np.broadcast_to(np.where(upper, 1.0, -1.0), (c, LANES)))
        masks.append(upper & ((r // m) == (t // m) - 1))
        m *= 2
    sums.append(r <= t)
    return (np.concatenate(sums, 0).astype(np.float32), np.stack(signs).astype(np.float32),
            np.stack(masks).astype(np.float32))


def _hgrn_kernel(q_ref, fz_ref, v_ref, g_ref, lbl_ref, on_ref, sums_ref, signs_ref, masks_ref, o_ref,
                 state, *, layer, chunk, heads):
    si = pl.program_id(2)
    n_levels = masks_ref.shape[0]
    tb = q_ref.shape[1]

    @pl.when(si == 0)
    def _():
        state[...] = jnp.zeros_like(state)

    logits = lbl_ref[...].astype(_F32)
    e = jnp.exp(logits - jnp.max(logits, axis=0, keepdims=True))
    sm = e / jnp.sum(e, axis=0, keepdims=True)
    lb_all = jnp.sum(sm[1:layer + 1, :], axis=0, keepdims=True) if layer >= 1 else jnp.zeros_like(sm[0:1, :])
    gain_all = on_ref[...]

    def head_chunk(hh, rows):
        cols = slice(hh * LANES, (hh + 1) * LANES)
        lb = lb_all[:, cols]
        gain = gain_all[:, cols]
        q = q_ref[0, rows, cols].astype(_F32)
        fz = fz_ref[0, rows, cols]
        v = v_ref[0, rows, cols]
        ez = jnp.exp(-jnp.abs(fz))
        rz = 1.0 / (1.0 + ez)
        pos = fz >= 0.0
        sig = jnp.where(pos, rz, ez * rz)
        nsig = jnp.where(pos, ez * rz, rz)
        lf = jnp.log(lb + (1.0 - lb) * sig)
        kk = (1.0 - lb) * nsig

        lf_hi, lf_lo = _split_bf16(lf, 2)
        d_all = jnp.dot(sums_ref[...], jnp.concatenate([lf_hi, lf_lo], axis=1),
                        preferred_element_type=_F32)
        d_all = d_all[:, 0:LANES] + d_all[:, LANES:2 * LANES]
        n_sum_levels = sums_ref.shape[0] // chunk - 1
        bcum = d_all[n_sum_levels * chunk:(n_sum_levels + 1) * chunk, :]

        q_b = q.astype(_BF16)
        a = lax.dot_general(q_b, kk.astype(_BF16), _NT, preferred_element_type=_F32) * masks_ref[0]
        for lvl in range(1, n_levels):
            if lvl <= n_sum_levels:
                expo = d_all[(lvl - 1) * chunk:lvl * chunk, :]
            else:
                m = 1 << (lvl - 1)
                ref = jnp.concatenate(
                    [jnp.broadcast_to(bcum[p0 + m - 1:p0 + m, :], (2 * m, LANES))
                     for p0 in range(0, chunk, 2 * m)], axis=0)
                expo = (bcum - ref) * signs_ref[lvl - 1 - n_sum_levels]
            dec = jnp.exp(expo)
            a_l = lax.dot_general((q * dec).astype(_BF16), (kk * dec).astype(_BF16), _NT,
                                  preferred_element_type=_F32)
            a = a + a_l * masks_ref[lvl]

        b_last = bcum[chunk - 1:chunk, :]
        st = state[hh]
        inter = lax.dot_general((q * jnp.exp(bcum)).astype(_BF16), st.astype(_BF16), _NT,
                                preferred_element_type=_F32)
        o = inter + jnp.dot(a.astype(_BF16), v, preferred_element_type=_F32)

        k_dec = (kk * jnp.exp(b_last - bcum)).astype(_BF16)
        v_t = v.astype(_F32).T.astype(_BF16)
        state[hh] = st * jnp.exp(b_last) + jnp.dot(v_t, k_dec, preferred_element_type=_F32)

        o = o * lax.rsqrt(jnp.mean(o * o, axis=-1, keepdims=True) + EPS)
        y = o * gain * _silu(g_ref[0, rows, cols].astype(_F32))
        o_ref[0, rows, cols] = y.astype(o_ref.dtype)

    def chunk_step(ci, carry):
        rows = pl.ds(pl.multiple_of(ci * chunk, chunk), chunk)
        for hh in range(heads):
            head_chunk(hh, rows)
        return carry

    lax.fori_loop(0, tb // chunk, chunk_step, 0)


def _hgrn_call(q, fz, v, g, lb_logits, onorm, *, layer, tb=1024, heads=8):
    b, s, w = q.shape
    nh = w // LANES
    tb = min(tb, s)
    chunk = min(HGRN_CHUNK, tb)
    assert s % tb == 0 and tb % chunk == 0 and nh % heads == 0
    sums, signs, masks = _hgrn_constants(chunk)
    sums = jnp.asarray(sums, _BF16)
    signs = jnp.asarray(signs, _F32)
    masks = jnp.asarray(masks, _F32)
    depth = lb_logits.shape[0]
    hw = heads * LANES
    head_tile = lambda bi, hi, si: (bi, si, hi)
    return pl.pallas_call(
        functools.partial(_hgrn_kernel, layer=layer, chunk=chunk, heads=heads),
        grid=(b, nh // heads, s // tb),
        in_specs=[pl.BlockSpec((1, tb, hw), head_tile),
                  pl.BlockSpec((1, tb, hw), head_tile),
                  pl.BlockSpec((1, tb, hw), head_tile),
                  pl.BlockSpec((1, tb, hw), head_tile),
                  pl.BlockSpec((depth, hw), lambda bi, hi, si: (0, hi)),
                  pl.BlockSpec((1, hw), lambda bi, hi, si: (0, hi)),
                  pl.BlockSpec(sums.shape, lambda bi, hi, si: (0, 0)),
                  pl.BlockSpec(signs.shape, lambda bi, hi, si: (0, 0, 0)),
                  pl.BlockSpec(masks.shape, lambda bi, hi, si: (0, 0, 0))],
        out_specs=pl.BlockSpec((1, tb, hw), head_tile),
        out_shape=jax.ShapeDtypeStruct((b, s, w), _BF16),
        scratch_shapes=[pltpu.VMEM((heads, LANES, LANES), _F32)],
        compiler_params=_params(("parallel", "parallel", "arbitrary")),
        name="hgrn2_recurrence",
    )(q, fz, v, g, lb_logits, onorm, sums, signs, masks)


def kernel(x, norm_gains, fox_w_in, fox_b_f, hgrn_w_in, hgrn_lb_logits, hgrn_onorm, w_out, final_gain):
    b, s, d = x.shape
    w = w_out.shape[1]
    nh = N_HEADS
    assert w == nh * LANES, "heads must be one lane tile wide"
    assert norm_gains.shape[0] == 2, "layer 0 is FoX, layer 1 is HGRN2"
    m = b * s
    x2 = x.reshape(m, d)

    wf = fox_w_in[0]
    wq, wk, wv = (wf[:, i * w:(i + 1) * w].astype(_BF16) for i in range(3))
    w_forget = jnp.pad(wf[:, 3 * w:3 * w + nh], ((0, 0), (0, LANES - nh))).astype(_BF16)
    wg = wf[:, 3 * w + nh:].astype(_BF16)
    q, k, v, g, flog = _proj_call(
        x2, norm_gains[0:1], [wq, wk, wv, wg], w_forget,
        ["scale", "none", "none", "none"], [_BF16] * 4, q_scale=float(LANES) ** -0.5 * LOG2E,
        name="fox_in_proj")
    b_pad = jnp.pad(fox_b_f[0:1], ((0, 0), (0, LANES - nh)))
    c = _decay_call(flog.reshape(b, s, LANES), b_pad)
    to3 = lambda t: t.reshape(b, s, w)
    y = _fox_attn_call(to3(q), to3(k), to3(v), to3(g), c)
    x2, h1 = _out_proj_call(y.reshape(m, w), w_out[0].astype(_BF16), x2, norm_gains[1:2],
                            emit_residual=True, norm_dtype=_BF16, name="fox_out_proj")

    wh = hgrn_w_in[0]
    whs = [wh[:, i * w:(i + 1) * w].astype(_BF16) for i in range(4)]
    q, fz, iv, g = _proj_call(h1, None, whs, None, ["silu", "none", "none", "none"],
                              [_BF16, _F32, _BF16, _BF16], name="hgrn_in_proj")
    y = _hgrn_call(to3(q), to3(fz), to3(iv), to3(g), hgrn_lb_logits, hgrn_onorm[0:1], layer=1)
    (out,) = _out_proj_call(y.reshape(m, w), w_out[1].astype(_BF16), x2, final_gain.reshape(1, d),
                            emit_residual=False, norm_dtype=x.dtype, name="hgrn_out_proj")
    return out.reshape(b, s, d)
```

```python
import functools

import numpy as np
import jax
import jax.numpy as jnp
from jax import lax
from jax.experimental import pallas as pl
from jax.experimental.pallas import tpu as pltpu

EPS = 1e-6
LOG2E = 1.4426950408889634
N_HEADS = 16
LANES = 128
SUBLANES = 8
HGRN_CHUNK = 128
VMEM_LIMIT = 56 * 1024 * 1024

_F32 = jnp.float32
_BF16 = jnp.bfloat16
_NT = (((1,), (1,)), ((), ()))


def _silu(x):
    return x / (1.0 + jnp.exp(-x))


def _split_bf16(x, terms):
    pieces = []
    rem = x
    for _ in range(terms):
        p = rem.astype(_BF16)
        pieces.append(p)
        rem = rem - p.astype(_F32)
    return pieces


def _params(semantics):
    return pltpu.CompilerParams(dimension_semantics=semantics, vmem_limit_bytes=VMEM_LIMIT)


def _proj_kernel(*refs, apply_norm, n_groups, epilogues, has_extra, q_scale):
    it = iter(refs)
    x_ref = next(it)
    gain_ref = next(it) if apply_norm else None
    w_refs = [next(it) for _ in range(n_groups)]
    wx_ref = next(it) if has_extra else None
    o_refs = [next(it) for _ in range(n_groups)]
    ox_ref = next(it) if has_extra else None
    h_scr = next(it) if apply_norm else None

    j = pl.program_id(1)

    if apply_norm:
        @pl.when(j == 0)
        def _():
            xf = x_ref[...].astype(_F32)
            y = xf * lax.rsqrt(jnp.mean(xf * xf, axis=-1, keepdims=True) + EPS)
            h_scr[...] = (y * gain_ref[...]).astype(_BF16)
            if has_extra:
                ox_ref[...] = jnp.dot(h_scr[...], wx_ref[...], preferred_element_type=_F32)
        h = h_scr[...]
    else:
        h = x_ref[...]
        if has_extra:
            @pl.when(j == 0)
            def _():
                ox_ref[...] = jnp.dot(h, wx_ref[...], preferred_element_type=_F32)

    for g in range(n_groups):
        acc = jnp.dot(h, w_refs[g][...], preferred_element_type=_F32)
        if epilogues[g] == "scale":
            acc = acc * q_scale
        elif epilogues[g] == "silu":
            acc = _silu(acc)
        o_refs[g][...] = acc.astype(o_refs[g].dtype)


def _proj_call(x, gain, weights, extra_w, epilogues, out_dtypes, *, q_scale=1.0, tm=1024, tn=512, name):
    m, d = x.shape
    n = weights[0].shape[1]
    tm, tn = min(tm, m), min(tn, n)
    assert m % tm == 0 and n % tn == 0
    apply_norm = gain is not None
    has_extra = extra_w is not None
    n_groups = len(weights)

    in_specs = [pl.BlockSpec((tm, d), lambda i, j: (i, 0))]
    args = [x]
    if apply_norm:
        in_specs.append(pl.BlockSpec((1, d), lambda i, j: (0, 0)))
        args.append(gain)
    for w in weights:
        in_specs.append(pl.BlockSpec((d, tn), lambda i, j: (0, j)))
        args.append(w)
    out_specs = [pl.BlockSpec((tm, tn), lambda i, j: (i, j)) for _ in weights]
    out_shape = [jax.ShapeDtypeStruct((m, n), dt) for dt in out_dtypes]
    if has_extra:
        in_specs.append(pl.BlockSpec((d, LANES), lambda i, j: (0, 0)))
        args.append(extra_w)
        out_specs.append(pl.BlockSpec((tm, LANES), lambda i, j: (i, 0)))
        out_shape.append(jax.ShapeDtypeStruct((m, LANES), _F32))
    scratch = [pltpu.VMEM((tm, d), _BF16)] if apply_norm else []

    kern = functools.partial(_proj_kernel, apply_norm=apply_norm, n_groups=n_groups,
                             epilogues=tuple(epilogues), has_extra=has_extra, q_scale=q_scale)
    return pl.pallas_call(
        kern,
        grid=(m // tm, n // tn),
        in_specs=in_specs,
        out_specs=out_specs,
        out_shape=out_shape,
        scratch_shapes=scratch,
        compiler_params=_params(("parallel", "arbitrary")),
        name=name,
    )(*args)


def _decay_kernel(f_ref, b_ref, c_ref, carry):
    t = pl.program_id(1)

    @pl.when(t == 0)
    def _():
        carry[...] = jnp.zeros_like(carry)

    z = f_ref[0] + b_ref[...]
    lf = jnp.minimum(z, 0.0) - jnp.log(1.0 + jnp.exp(-jnp.abs(z)))
    n = lf.shape[0]
    row = lax.broadcasted_iota(jnp.int32, (n, n), 0)
    col = lax.broadcasted_iota(jnp.int32, (n, n), 1)
    tri = (col <= row).astype(_BF16)
    cs = carry[0:1, :]
    for piece in _split_bf16(lf, 3):
        cs = cs + jnp.dot(tri, piece, preferred_element_type=_F32)
    c_ref[0] = cs
    carry[0:1, :] = cs[n - 1:n, :]


def _decay_call(flog, b_pad, *, tb=512):
    b, s, _ = flog.shape
    tb = min(tb, s)
    assert s % tb == 0
    return pl.pallas_call(
        _decay_kernel,
        grid=(b, s // tb),
        in_specs=[pl.BlockSpec((1, tb, LANES), lambda bi, ti: (bi, ti, 0)),
                  pl.BlockSpec((1, LANES), lambda bi, ti: (0, 0))],
        out_specs=pl.BlockSpec((1, tb, LANES), lambda bi, ti: (bi, ti, 0)),
        out_shape=jax.ShapeDtypeStruct((b, s, LANES), _F32),
        scratch_shapes=[pltpu.VMEM((8, LANES), _F32)],
        compiler_params=_params(("parallel", "arbitrary")),
        name="fox_decay_cumsum",
    )(flog, b_pad)


def _fox_attn_kernel(q_ref, k_ref, v_ref, g_ref, cq_ref, ck_ref, o_ref,
                     kaug, vaug, qaug, m_scr, acc_scr, s_buf, p_buf, a_buf,
                     *, tq, tk, rows_per_group, kb_rows):
    h = pl.program_id(1)
    qi = pl.program_id(2)
    s_len = k_ref.shape[1]
    rg = rows_per_group
    n_groups = tq // rg
    lane = lax.broadcasted_iota(jnp.int32, (1, LANES), 1)

    def head_column(c):
        return jnp.sum(jnp.where(lane == h, c, 0.0), axis=1, keepdims=True) * LOG2E

    @pl.when(qi == 0)
    def _():
        ones_col = jnp.where(lane == 0, 1.0, 0.0).astype(_BF16)

        def build(r, carry):
            rows = pl.ds(pl.multiple_of(r * kb_rows, kb_rows), kb_rows)
            hi, mid, lo = [p.astype(_F32) for p in _split_bf16(head_column(ck_ref[0, rows, :]), 3)]
            aug = jnp.where(lane == 0, -hi, jnp.where(lane == 1, -mid, jnp.where(
                lane == 2, -lo, jnp.where(lane < 6, 1.0, 0.0))))
            kaug[rows, 0:LANES] = k_ref[0, rows, :]
            kaug[rows, LANES:2 * LANES] = aug.astype(_BF16)
            vaug[rows, 0:LANES] = v_ref[0, rows, :]
            vaug[rows, LANES:2 * LANES] = jnp.broadcast_to(ones_col, (kb_rows, LANES))
            return carry
        lax.fori_loop(0, s_len // kb_rows, build, 0)

    hi, mid, lo = [p.astype(_F32) for p in _split_bf16(head_column(cq_ref[0]), 3)]
    qaug[:, 0:LANES] = q_ref[0]
    qaug[:, LANES:2 * LANES] = jnp.where(lane < 3, 1.0, jnp.where(lane == 3, hi, jnp.where(
        lane == 4, mid, jnp.where(lane == 5, lo, 0.0)))).astype(_BF16)
    m_scr[...] = jnp.full_like(m_scr, -jnp.inf)
    acc_scr[...] = jnp.zeros_like(acc_scr)

    n_full = 2 * qi
    all_rows = slice(0, tq)
    late_rows = slice(tk, tq)

    def key_block(j):
        return pl.ds(pl.multiple_of(j * tk, tk), tk)

    def scores(j, slot, rows=all_rows):
        s_buf[slot, rows, :] = lax.dot_general(qaug[rows, :], kaug[key_block(j), :], _NT,
                                               preferred_element_type=_F32)

    def softmax(slot, diag=None):
        for g in range(n_groups):
            row0 = g * rg
            key0 = 0 if diag is None else diag * tk
            if diag is not None and row0 + rg <= key0:
                continue
            grp = slice(row0, row0 + rg)
            s = s_buf[slot, grp, :]
            if diag is not None and row0 < key0 + tk - 1:
                r = lax.broadcasted_iota(jnp.int32, s.shape, 0) + row0
                c = lax.broadcasted_iota(jnp.int32, s.shape, 1) + key0
                s = jnp.where(c <= r, s, -jnp.inf)
            m_prev = m_scr[grp, :]
            m_new = jnp.maximum(m_prev, jnp.max(s, axis=1, keepdims=True))
            a_buf[slot, grp, :] = jnp.exp2(m_prev - m_new)
            p_buf[slot, grp, :] = jnp.exp2(s - m_new).astype(_BF16)
            m_scr[grp, :] = m_new

    def values(j, slot, rows=all_rows):
        acc_scr[rows, :] = a_buf[slot, rows, :] * acc_scr[rows, :] + jnp.dot(
            p_buf[slot, rows, :], vaug[key_block(j), :], preferred_element_type=_F32)

    scores(0, 0)
    p_buf[1] = jnp.zeros(p_buf.shape[1:], p_buf.dtype)
    a_buf[1] = jnp.ones(a_buf.shape[1:], a_buf.dtype)

    def pair(i, carry):
        j0 = 2 * i
        values(jnp.maximum(j0 - 1, 0), 1)
        softmax(0)
        scores(j0 + 1, 1)
        values(j0, 0)
        softmax(1)
        scores(j0 + 2, 0)
        return carry
    lax.fori_loop(0, qi, pair, 0)

    values(jnp.maximum(n_full - 1, 0), 1)
    softmax(0, diag=0)
    scores(n_full + 1, 1, late_rows)
    values(n_full, 0)
    softmax(1, diag=1)
    values(n_full + 1, 1, late_rows)

    acc = acc_scr[...]
    o = acc[:, 0:LANES] / acc[:, LANES:LANES + 1]
    o_ref[0] = (o * _silu(g_ref[0].astype(_F32))).astype(o_ref.dtype)


def _fox_attn_call(q, k, v, g, c, *, tq=1024, tk=512, rows_per_group=256):
    b, s, w = q.shape
    nh = w // LANES
    tq = min(tq, s)
    tk = min(tk, tq // 2)
    rows_per_group = min(rows_per_group, tq)
    assert s % tq == 0 and tq % rows_per_group == 0
    assert tq == 2 * tk, "the diagonal tile is handled as exactly two key blocks"
    kern = functools.partial(_fox_attn_kernel, tq=tq, tk=tk, rows_per_group=rows_per_group,
                             kb_rows=min(512, s))
    head_tile = lambda bi, hi, qi: (bi, qi, hi)
    head_full = lambda bi, hi, qi: (bi, 0, hi)
    return pl.pallas_call(
        kern,
        grid=(b, nh, s // tq),
        in_specs=[pl.BlockSpec((1, tq, LANES), head_tile),
                  pl.BlockSpec((1, s, LANES), head_full),
                  pl.BlockSpec((1, s, LANES), head_full),
                  pl.BlockSpec((1, tq, LANES), head_tile),
                  pl.BlockSpec((1, tq, LANES), lambda bi, hi, qi: (bi, qi, 0)),
                  pl.BlockSpec((1, s, LANES), lambda bi, hi, qi: (bi, 0, 0))],
        out_specs=pl.BlockSpec((1, tq, LANES), head_tile),
        out_shape=jax.ShapeDtypeStruct((b, s, w), _BF16),
        scratch_shapes=[pltpu.VMEM((s, 2 * LANES), _BF16),
                        pltpu.VMEM((s, 2 * LANES), _BF16),
                        pltpu.VMEM((tq, 2 * LANES), _BF16),
                        pltpu.VMEM((tq, 1), _F32),
                        pltpu.VMEM((tq, 2 * LANES), _F32),
                        pltpu.VMEM((2, tq, tk), _F32),
                        pltpu.VMEM((2, tq, tk), _BF16),
                        pltpu.VMEM((2, tq, 1), _F32)],
        compiler_params=_params(("parallel", "parallel", "arbitrary")),
        name="fox_attention",
    )(q, k, v, g, c, c)


def _out_proj_kernel(y_ref, w_ref, x_ref, gain_ref, *o_refs, emit_residual):
    xn = x_ref[...] + jnp.dot(y_ref[...], w_ref[...], preferred_element_type=_F32)
    normed = xn * lax.rsqrt(jnp.mean(xn * xn, axis=-1, keepdims=True) + EPS) * gain_ref[...]
    if emit_residual:
        o_refs[0][...] = xn
        o_refs[1][...] = normed.astype(o_refs[1].dtype)
    else:
        o_refs[0][...] = normed.astype(o_refs[0].dtype)


def _out_proj_call(y, w, x, gain, *, emit_residual, norm_dtype, tm=256, name):
    m, wd = y.shape
    d = w.shape[1]
    tm = min(tm, m)
    assert m % tm == 0
    row_tile = lambda i: (i, 0)
    out_specs = [pl.BlockSpec((tm, d), row_tile)]
    out_shape = [jax.ShapeDtypeStruct((m, d), norm_dtype)]
    if emit_residual:
        out_specs.insert(0, pl.BlockSpec((tm, d), row_tile))
        out_shape.insert(0, jax.ShapeDtypeStruct((m, d), _F32))
    return pl.pallas_call(
        functools.partial(_out_proj_kernel, emit_residual=emit_residual),
        grid=(m // tm,),
        in_specs=[pl.BlockSpec((tm, wd), row_tile),
                  pl.BlockSpec((wd, d), lambda i: (0, 0)),
                  pl.BlockSpec((tm, d), row_tile),
                  pl.BlockSpec((1, d), lambda i: (0, 0))],
        out_specs=out_specs,
        out_shape=out_shape,
        compiler_params=_params(("parallel",)),
        name=name,
    )(y, w, x, gain)


def _hgrn_constants(c):
    t = np.arange(c)[:, None]
    r = np.arange(c)[None, :]
    sums, signs, masks = [], [], [np.eye(c, dtype=bool)]
    m = 1
    while m < c:
        same = (t // m) == (r // m)
        upper = ((t // m) % 2) == 1
        if 2 * m < SUBLANES:
            sums.append(same & np.where(upper, r <= t, r > t))
        else:
            signs.append(np.broadcast_to(np.where(upper, 1.0, -1.0), (c, LANES)))
        masks.append(upper & ((r // m) == (t // m) - 1))
        m *= 2
    sums.append(r <= t)
    return (np.concatenate(sums, 0).astype(np.float32), np.stack(signs).astype(np.float32),
            np.stack(masks).astype(np.float32))


def _hgrn_kernel(q_ref, fz_ref, v_ref, g_ref, lbl_ref, on_ref, sums_ref, signs_ref, masks_ref, o_ref,
                 state, *, layer, chunk, heads):
    si = pl.program_id(2)
    n_levels = masks_ref.shape[0]
    tb = q_ref.shape[1]

    @pl.when(si == 0)
    def _():
        state[...] = jnp.zeros_like(state)

    logits = lbl_ref[...].astype(_F32)
    e = jnp.exp(logits - jnp.max(logits, axis=0, keepdims=True))
    sm = e / jnp.sum(e, axis=0, keepdims=True)
    lb_all = jnp.sum(sm[1:layer + 1, :], axis=0, keepdims=True) if layer >= 1 else jnp.zeros_like(sm[0:1, :])
    gain_all = on_ref[...]

    def head_chunk(hh, rows):
        cols = slice(hh * LANES, (hh + 1) * LANES)
        lb = lb_all[:, cols]
        gain = gain_all[:, cols]
        q = q_ref[0, rows, cols].astype(_F32)
        fz = fz_ref[0, rows, cols]
        v = v_ref[0, rows, cols]
        ez = jnp.exp(-jnp.abs(fz))
        rz = 1.0 / (1.0 + ez)
        pos = fz >= 0.0
        sig = jnp.where(pos, rz, ez * rz)
        nsig = jnp.where(pos, ez * rz, rz)
        lf = jnp.log(lb + (1.0 - lb) * sig)
        kk = (1.0 - lb) * nsig

        lf_hi, lf_lo = _split_bf16(lf, 2)
        d_all = jnp.dot(sums_ref[...], jnp.concatenate([lf_hi, lf_lo], axis=1),
                        preferred_element_type=_F32)
        d_all = d_all[:, 0:LANES] + d_all[:, LANES:2 * LANES]
        n_sum_levels = sums_ref.shape[0] // chunk - 1
        bcum = d_all[n_sum_levels * chunk:(n_sum_levels + 1) * chunk, :]

        q_b = q.astype(_BF16)
        a = lax.dot_general(q_b, kk.astype(_BF16), _NT, preferred_element_type=_F32) * masks_ref[0]
        for lvl in range(1, n_levels):
            if lvl <= n_sum_levels:
                expo = d_all[(lvl - 1) * chunk:lvl * chunk, :]
            else:
                m = 1 << (lvl - 1)
                ref = jnp.concatenate(
                    [jnp.broadcast_to(bcum[p0 + m - 1:p0 + m, :], (2 * m, LANES))
                     for p0 in range(0, chunk, 2 * m)], axis=0)
                expo = (bcum - ref) * signs_ref[lvl - 1 - n_sum_levels]
            dec = jnp.exp(expo)
            a_l = lax.dot_general((q * dec).astype(_BF16), (kk * dec).astype(_BF16), _NT,
                                  preferred_element_type=_F32)
            a = a + a_l * masks_ref[lvl]

        b_last = bcum[chunk - 1:chunk, :]
        st = state[hh]
        inter = lax.dot_general((q * jnp.exp(bcum)).astype(_BF16), st.astype(_BF16), _NT,
                                preferred_element_type=_F32)
        o = inter + jnp.dot(a.astype(_BF16), v, preferred_element_type=_F32)

        k_dec = (kk * jnp.exp(b_last - bcum)).astype(_BF16)
        v_t = v.astype(_F32).T.astype(_BF16)
        state[hh] = st * jnp.exp(b_last) + jnp.dot(v_t, k_dec, preferred_element_type=_F32)

        o = o * lax.rsqrt(jnp.mean(o * o, axis=-1, keepdims=True) + EPS)
        y = o * gain * _silu(g_ref[0, rows, cols].astype(_F32))
        o_ref[0, rows, cols] = y.astype(o_ref.dtype)

    def chunk_step(ci, carry):
        rows = pl.ds(pl.multiple_of(ci * chunk, chunk), chunk)
        for hh in range(heads):
            head_chunk(hh, rows)
        return carry

    lax.fori_loop(0, tb // chunk, chunk_step, 0)


def _hgrn_call(q, fz, v, g, lb_logits, onorm, *, layer, tb=1024, heads=8):
    b, s, w = q.shape
    nh = w // LANES
    tb = min(tb, s)
    chunk = min(HGRN_CHUNK, tb)
    assert s % tb == 0 and tb % chunk == 0 and nh % heads == 0
    sums, signs, masks = _hgrn_constants(chunk)
    sums = jnp.asarray(sums, _BF16)
    signs = jnp.asarray(signs, _F32)
    masks = jnp.asarray(masks, _F32)
    depth = lb_logits.shape[0]
    hw = heads * LANES
    head_tile = lambda bi, hi, si: (bi, si, hi)
    return pl.pallas_call(
        functools.partial(_hgrn_kernel, layer=layer, chunk=chunk, heads=heads),
        grid=(b, nh // heads, s // tb),
        in_specs=[pl.BlockSpec((1, tb, hw), head_tile),
                  pl.BlockSpec((1, tb, hw), head_tile),
                  pl.BlockSpec((1, tb, hw), head_tile),
                  pl.BlockSpec((1, tb, hw), head_tile),
                  pl.BlockSpec((depth, hw), lambda bi, hi, si: (0, hi)),
                  pl.BlockSpec((1, hw), lambda bi, hi, si: (0, hi)),
                  pl.BlockSpec(sums.shape, lambda bi, hi, si: (0, 0)),
                  pl.BlockSpec(signs.shape, lambda bi, hi, si: (0, 0, 0)),
                  pl.BlockSpec(masks.shape, lambda bi, hi, si: (0, 0, 0))],
        out_specs=pl.BlockSpec((1, tb, hw), head_tile),
        out_shape=jax.ShapeDtypeStruct((b, s, w), _BF16),
        scratch_shapes=[pltpu.VMEM((heads, LANES, LANES), _F32)],
        compiler_params=_params(("parallel", "parallel", "arbitrary")),
        name="hgrn2_recurrence",
    )(q, fz, v, g, lb_logits, onorm, sums, signs, masks)


def kernel(x, norm_gains, fox_w_in, fox_b_f, hgrn_w_in, hgrn_lb_logits, hgrn_onorm, w_out, final_gain):
    b, s, d = x.shape
    w = w_out.shape[1]
    nh = N_HEADS
    assert w == nh * LANES, "heads must be one lane tile wide"
    assert norm_gains.shape[0] == 2, "layer 0 is FoX, layer 1 is HGRN2"
    m = b * s
    x2 = x.reshape(m, d)

    wf = fox_w_in[0]
    wq, wk, wv = (wf[:, i * w:(i + 1) * w].astype(_BF16) for i in range(3))
    w_forget = jnp.pad(wf[:, 3 * w:3 * w + nh], ((0, 0), (0, LANES - nh))).astype(_BF16)
    wg = wf[:, 3 * w + nh:].astype(_BF16)
    q, k, v, g, flog = _proj_call(
        x2, norm_gains[0:1], [wq, wk, wv, wg], w_forget,
        ["scale", "none", "none", "none"], [_BF16] * 4, q_scale=float(LANES) ** -0.5 * LOG2E,
        name="fox_in_proj")
    b_pad = jnp.pad(fox_b_f[0:1], ((0, 0), (0, LANES - nh)))
    c = _decay_call(flog.reshape(b, s, LANES), b_pad)
    to3 = lambda t: t.reshape(b, s, w)
    y = _fox_attn_call(to3(q), to3(k), to3(v), to3(g), c)
    x2, h1 = _out_proj_call(y.reshape(m, w), w_out[0].astype(_BF16), x2, norm_gains[1:2],
                            emit_residual=True, norm_dtype=_BF16, name="fox_out_proj")

    wh = hgrn_w_in[0]
    whs = [wh[:, i * w:(i + 1) * w].astype(_BF16) for i in range(4)]
    q, fz, iv, g = _proj_call(h1, None, whs, None, ["silu", "none", "none", "none"],
                              [_BF16, _F32, _BF16, _BF16], name="hgrn_in_proj")
    y = _hgrn_call(to3(q), to3(fz), to3(iv), to3(g), hgrn_lb_logits, hgrn_onorm[0:1], layer=1)
    (out,) = _out_proj_call(y.reshape(m, w), w_out[1].astype(_BF16), x2, final_gain.reshape(1, d),
                            emit_residual=False, norm_dtype=x.dtype, name="hgrn_out_proj")
    return out.reshape(b, s, d)
```

```python
import functools

import numpy as np
import jax
import jax.numpy as jnp
from jax import lax
from jax.experimental import pallas as pl
from jax.experimental.pallas import tpu as pltpu

EPS = 1e-6
LOG2E = 1.4426950408889634
N_HEADS = 16
LANES = 128
SUBLANES = 8
HGRN_CHUNK = 128
VMEM_LIMIT = 56 * 1024 * 1024

_F32 = jnp.float32
_BF16 = jnp.bfloat16
_NT = (((1,), (1,)), ((), ()))


def _silu(x):
    return x / (1.0 + jnp.exp(-x))


def _split_bf16(x, terms):
    pieces = []
    rem = x
    for _ in range(terms):
        p = rem.astype(_BF16)
        pieces.append(p)
        rem = rem - p.astype(_F32)
    return pieces


def _params(semantics):
    return pltpu.CompilerParams(dimension_semantics=semantics, vmem_limit_bytes=VMEM_LIMIT)


def _proj_kernel(*refs, apply_norm, n_groups, epilogues, has_extra, q_scale):
    it = iter(refs)
    x_ref = next(it)
    gain_ref = next(it) if apply_norm else None
    w_refs = [next(it) for _ in range(n_groups)]
    wx_ref = next(it) if has_extra else None
    o_refs = [next(it) for _ in range(n_groups)]
    ox_ref = next(it) if has_extra else None
    h_scr = next(it) if apply_norm else None

    j = pl.program_id(1)

    if apply_norm:
        @pl.when(j == 0)
        def _():
            xf = x_ref[...].astype(_F32)
            y = xf * lax.rsqrt(jnp.mean(xf * xf, axis=-1, keepdims=True) + EPS)
            h_scr[...] = (y * gain_ref[...]).astype(_BF16)
            if has_extra:
                ox_ref[...] = jnp.dot(h_scr[...], wx_ref[...], preferred_element_type=_F32)
        h = h_scr[...]
    else:
        h = x_ref[...]
        if has_extra:
            @pl.when(j == 0)
            def _():
                ox_ref[...] = jnp.dot(h, wx_ref[...], preferred_element_type=_F32)

    for g in range(n_groups):
        acc = jnp.dot(h, w_refs[g][...], preferred_element_type=_F32)
        if epilogues[g] == "scale":
            acc = acc * q_scale
        elif epilogues[g] == "silu":
            acc = _silu(acc)
        o_refs[g][...] = acc.astype(o_refs[g].dtype)


def _proj_call(x, gain, weights, extra_w, epilogues, out_dtypes, *, q_scale=1.0, tm=1024, tn=512, name):
    m, d = x.shape
    n = weights[0].shape[1]
    tm, tn = min(tm, m), min(tn, n)
    assert m % tm == 0 and n % tn == 0
    apply_norm = gain is not None
    has_extra = extra_w is not None
    n_groups = len(weights)

    in_specs = [pl.BlockSpec((tm, d), lambda i, j: (i, 0))]
    args = [x]
    if apply_norm:
        in_specs.append(pl.BlockSpec((1, d), lambda i, j: (0, 0)))
        args.append(gain)
    for w in weights:
        in_specs.append(pl.BlockSpec((d, tn), lambda i, j: (0, j)))
        args.append(w)
    out_specs = [pl.BlockSpec((tm, tn), lambda i, j: (i, j)) for _ in weights]
    out_shape = [jax.ShapeDtypeStruct((m, n), dt) for dt in out_dtypes]
    if has_extra:
        in_specs.append(pl.BlockSpec((d, LANES), lambda i, j: (0, 0)))
        args.append(extra_w)
        out_specs.append(pl.BlockSpec((tm, LANES), lambda i, j: (i, 0)))
        out_shape.append(jax.ShapeDtypeStruct((m, LANES), _F32))
    scratch = [pltpu.VMEM((tm, d), _BF16)] if apply_norm else []

    kern = functools.partial(_proj_kernel, apply_norm=apply_norm, n_groups=n_groups,
                             epilogues=tuple(epilogues), has_extra=has_extra, q_scale=q_scale)
    return pl.pallas_call(
        kern,
        grid=(m // tm, n // tn),
        in_specs=in_specs,
        out_specs=out_specs,
        out_shape=out_shape,
        scratch_shapes=scratch,
        compiler_params=_params(("parallel", "arbitrary")),
        name=name,
    )(*args)


def _decay_kernel(f_ref, b_ref, c_ref, carry):
    t = pl.program_id(1)

    @pl.when(t == 0)
    def _():
        carry[...] = jnp.zeros_like(carry)

    z = f_ref[0] + b_ref[...]
    lf = jnp.minimum(z, 0.0) - jnp.log(1.0 + jnp.exp(-jnp.abs(z)))
    n = lf.shape[0]
    row = lax.broadcasted_iota(jnp.int32, (n, n), 0)
    col = lax.broadcasted_iota(jnp.int32, (n, n), 1)
    tri = (col <= row).astype(_BF16)
    cs = carry[0:1, :]
    for piece in _split_bf16(lf, 3):
        cs = cs + jnp.dot(tri, piece, preferred_element_type=_F32)
    c_ref[0] = cs
    carry[0:1, :] = cs[n - 1:n, :]


def _decay_call(flog, b_pad, *, tb=512):
    b, s, _ = flog.shape
    tb = min(tb, s)
    assert s % tb == 0
    return pl.pallas_call(
        _decay_kernel,
        grid=(b, s // tb),
        in_specs=[pl.BlockSpec((1, tb, LANES), lambda bi, ti: (bi, ti, 0)),
                  pl.BlockSpec((1, LANES), lambda bi, ti: (0, 0))],
        out_specs=pl.BlockSpec((1, tb, LANES), lambda bi, ti: (bi, ti, 0)),
        out_shape=jax.ShapeDtypeStruct((b, s, LANES), _F32),
        scratch_shapes=[pltpu.VMEM((8, LANES), _F32)],
        compiler_params=_params(("parallel", "arbitrary")),
        name="fox_decay_cumsum",
    )(flog, b_pad)


def _fox_attn_kernel(q_ref, k_ref, v_ref, g_ref, cq_ref, ck_ref, o_ref,
                     kaug, vaug, qaug, m_scr, acc_scr, s_buf, p_buf, a_buf,
                     *, tq, tk, rows_per_group, kb_rows):
    h = pl.program_id(1)
    qi = pl.program_id(2)
    s_len = k_ref.shape[1]
    rg = rows_per_group
    n_groups = tq // rg
    lane = lax.broadcasted_iota(jnp.int32, (1, LANES), 1)

    def head_column(c):
        return jnp.sum(jnp.where(lane == h, c, 0.0), axis=1, keepdims=True) * LOG2E

    @pl.when(qi == 0)
    def _():
        ones_col = jnp.where(lane == 0, 1.0, 0.0).astype(_BF16)

        def build(r, carry):
            rows = pl.ds(pl.multiple_of(r * kb_rows, kb_rows), kb_rows)
            hi, mid, lo = [p.astype(_F32) for p in _split_bf16(head_column(ck_ref[0, rows, :]), 3)]
            aug = jnp.where(lane == 0, -hi, jnp.where(lane == 1, -mid, jnp.where(
                lane == 2, -lo, jnp.where(lane < 6, 1.0, 0.0))))
            kaug[rows, 0:LANES] = k_ref[0, rows, :]
            kaug[rows, LANES:2 * LANES] = aug.astype(_BF16)
            vaug[rows, 0:LANES] = v_ref[0, rows, :]
            vaug[rows, LANES:2 * LANES] = jnp.broadcast_to(ones_col, (kb_rows, LANES))
            return carry
        lax.fori_loop(0, s_len // kb_rows, build, 0)

    hi, mid, lo = [p.astype(_F32) for p in _split_bf16(head_column(cq_ref[0]), 3)]
    qaug[:, 0:LANES] = q_ref[0]
    qaug[:, LANES:2 * LANES] = jnp.where(lane < 3, 1.0, jnp.where(lane == 3, hi, jnp.where(
        lane == 4, mid, jnp.where(lane == 5, lo, 0.0)))).astype(_BF16)
    m_scr[...] = jnp.full_like(m_scr, -jnp.inf)
    acc_scr[...] = jnp.zeros_like(acc_scr)

    n_full = 2 * qi
    all_rows = slice(0, tq)
    late_rows = slice(tk, tq)

    def key_block(j):
        return pl.ds(pl.multiple_of(j * tk, tk), tk)

    def scores(j, slot, rows=all_rows):
        s_buf[slot, rows, :] = lax.dot_general(qaug[rows, :], kaug[key_block(j), :], _NT,
                                               preferred_element_type=_F32)

    def softmax(slot, diag=None):
        for g in range(n_groups):
            row0 = g * rg
            key0 = 0 if diag is None else diag * tk
            if diag is not None and row0 + rg <= key0:
                continue
            grp = slice(row0, row0 + rg)
            s = s_buf[slot, grp, :]
            if diag is not None and row0 < key0 + tk - 1:
                r = lax.broadcasted_iota(jnp.int32, s.shape, 0) + row0
                c = lax.broadcasted_iota(jnp.int32, s.shape, 1) + key0
                s = jnp.where(c <= r, s, -jnp.inf)
            m_prev = m_scr[grp, :]
            m_new = jnp.maximum(m_prev, jnp.max(s, axis=1, keepdims=True))
            a_buf[slot, grp, :] = jnp.exp2(m_prev - m_new)
            p_buf[slot, grp, :] = jnp.exp2(s - m_new).astype(_BF16)
            m_scr[grp, :] = m_new

    def values(j, slot, rows=all_rows):
        acc_scr[rows, :] = a_buf[slot, rows, :] * acc_scr[rows, :] + jnp.dot(
            p_buf[slot, rows, :], vaug[key_block(j), :], preferred_element_type=_F32)

    scores(0, 0)
    p_buf[1] = jnp.zeros(p_buf.shape[1:], p_buf.dtype)
    a_buf[1] = jnp.ones(a_buf.shape[1:], a_buf.dtype)

    def pair(i, carry):
        j0 = 2 * i
        values(jnp.maximum(j0 - 1, 0), 1)
        softmax(0)
        scores(j0 + 1, 1)
        values(j0, 0)
        softmax(1)
        scores(j0 + 2, 0)
        return carry
    lax.fori_loop(0, qi, pair, 0)

    values(jnp.maximum(n_full - 1, 0), 1)
    softmax(0, diag=0)
    scores(n_full + 1, 1, late_rows)
    values(n_full, 0)
    softmax(1, diag=1)
    values(n_full + 1, 1, late_rows)

    acc = acc_scr[...]
    o = acc[:, 0:LANES] / acc[:, LANES:LANES + 1]
    o_ref[0] = (o * _silu(g_ref[0].astype(_F32))).astype(o_ref.dtype)


def _fox_attn_call(q, k, v, g, c, *, tq=1024, tk=512, rows_per_group=256):
    b, s, w = q.shape
    nh = w // LANES
    tq = min(tq, s)
    tk = min(tk, tq // 2)
    rows_per_group = min(rows_per_group, tq)
    assert s % tq == 0 and tq % rows_per_group == 0
    assert tq == 2 * tk, "the diagonal tile is handled as exactly two key blocks"
    kern = functools.partial(_fox_attn_kernel, tq=tq, tk=tk, rows_per_group=rows_per_group,
                             kb_rows=min(512, s))
    head_tile = lambda bi, hi, qi: (bi, qi, hi)
    head_full = lambda bi, hi, qi: (bi, 0, hi)
    return pl.pallas_call(
        kern,
        grid=(b, nh, s // tq),
        in_specs=[pl.BlockSpec((1, tq, LANES), head_tile),
                  pl.BlockSpec((1, s, LANES), head_full),
                  pl.BlockSpec((1, s, LANES), head_full),
                  pl.BlockSpec((1, tq, LANES), head_tile),
                  pl.BlockSpec((1, tq, LANES), lambda bi, hi, qi: (bi, qi, 0)),
                  pl.BlockSpec((1, s, LANES), lambda bi, hi, qi: (bi, 0, 0))],
        out_specs=pl.BlockSpec((1, tq, LANES), head_tile),
        out_shape=jax.ShapeDtypeStruct((b, s, w), _BF16),
        scratch_shapes=[pltpu.VMEM((s, 2 * LANES), _BF16),
                        pltpu.VMEM((s, 2 * LANES), _BF16),
                        pltpu.VMEM((tq, 2 * LANES), _BF16),
                        pltpu.VMEM((tq, 1), _F32),
                        pltpu.VMEM((tq, 2 * LANES), _F32),
                        pltpu.VMEM((2, tq, tk), _F32),
                        pltpu.VMEM((2, tq, tk), _BF16),
                        pltpu.VMEM((2, tq, 1), _F32)],
        compiler_params=_params(("parallel", "parallel", "arbitrary")),
        name="fox_attention",
    )(q, k, v, g, c, c)


def _out_proj_kernel(y_ref, w_ref, x_ref, gain_ref, *o_refs, emit_residual):
    xn = x_ref[...] + jnp.dot(y_ref[...], w_ref[...], preferred_element_type=_F32)
    normed = xn * lax.rsqrt(jnp.mean(xn * xn, axis=-1, keepdims=True) + EPS) * gain_ref[...]
    if emit_residual:
        o_refs[0][...] = xn
        o_refs[1][...] = normed.astype(o_refs[1].dtype)
    else:
        o_refs[0][...] = normed.astype(o_refs[0].dtype)


def _out_proj_call(y, w, x, gain, *, emit_residual, norm_dtype, tm=256, name):
    m, wd = y.shape
    d = w.shape[1]
    tm = min(tm, m)
    assert m % tm == 0
    row_tile = lambda i: (i, 0)
    out_specs = [pl.BlockSpec((tm, d), row_tile)]
    out_shape = [jax.ShapeDtypeStruct((m, d), norm_dtype)]
    if emit_residual:
        out_specs.insert(0, pl.BlockSpec((tm, d), row_tile))
        out_shape.insert(0, jax.ShapeDtypeStruct((m, d), _F32))
    return pl.pallas_call(
        functools.partial(_out_proj_kernel, emit_residual=emit_residual),
        grid=(m // tm,),
        in_specs=[pl.BlockSpec((tm, wd), row_tile),
                  pl.BlockSpec((wd, d), lambda i: (0, 0)),
                  pl.BlockSpec((tm, d), row_tile),
                  pl.BlockSpec((1, d), lambda i: (0, 0))],
        out_specs=out_specs,
        out_shape=out_shape,
        compiler_params=_params(("parallel",)),
        name=name,
    )(y, w, x, gain)


def _hgrn_constants(c):
    t = np.arange(c)[:, None]
    r = np.arange(c)[None, :]
    sums, signs, masks = [], [], [np.eye(c, dtype=bool)]
    m = 1
    while m < c:
        same = (t // m) == (r // m)
        upper = ((t // m) % 2) == 1
        if 2 * m < SUBLANES:
            sums.append(same & np.where(upper, r <= t, r > t))
        else:
            signs.append(np.broadcast_to(np.where(upper, 1.0, -1.0), (c, LANES)))
        masks.append(upper & ((r // m) == (t // m) - 1))
        m *= 2
    sums.append(r <= t)
    return (np.concatenate(sums, 0).astype(np.float32), np.stack(signs).astype(np.float32),
            np.stack(masks).astype(np.float32))


def _hgrn_kernel(q_ref, fz_ref, v_ref, g_ref, lbl_ref, on_ref, sums_ref, signs_ref, masks_ref, o_ref,
                 state, *, layer, chunk, heads):
    si = pl.program_id(2)
    n_levels = masks_ref.shape[0]
    tb = q_ref.shape[1]

    @pl.when(si == 0)
    def _():
        state[...] = jnp.zeros_like(state)

    logits = lbl_ref[...].astype(_F32)
    e = jnp.exp(logits - jnp.max(logits, axis=0, keepdims=True))
    sm = e / jnp.sum(e, axis=0, keepdims=True)
    lb_all = jnp.sum(sm[1:layer + 1, :], axis=0, keepdims=True) if layer >= 1 else jnp.zeros_like(sm[0:1, :])
    gain_all = on_ref[...]

    def head_chunk(hh, rows):
        cols = slice(hh * LANES, (hh + 1) * LANES)
        lb = lb_all[:, cols]
        gain = gain_all[:, cols]
        q = q_ref[0, rows, cols].astype(_F32)
        fz = fz_ref[0, rows, cols]
        v = v_ref[0, rows, cols]
        ez = jnp.exp(-jnp.abs(fz))
        rz = 1.0 / (1.0 + ez)
        pos = fz >= 0.0
        sig = jnp.where(pos, rz, ez * rz)
        nsig = jnp.where(pos, ez * rz, rz)
        lf = jnp.log(lb + (1.0 - lb) * sig)
        kk = (1.0 - lb) * nsig

        lf_hi, lf_lo = _split_bf16(lf, 2)
        d_all = jnp.dot(sums_ref[...], jnp.concatenate([lf_hi, lf_lo], axis=1),
                        preferred_element_type=_F32)
        d_all = d_all[:, 0:LANES] + d_all[:, LANES:2 * LANES]
        n_sum_levels = sums_ref.shape[0] // chunk - 1
        bcum = d_all[n_sum_levels * chunk:(n_sum_levels + 1) * chunk, :]

        q_b = q.astype(_BF16)
        a = lax.dot_general(q_b, kk.astype(_BF16), _NT, preferred_element_type=_F32) * masks_ref[0]
        for lvl in range(1, n_levels):
            if lvl <= n_sum_levels:
                expo = d_all[(lvl - 1) * chunk:lvl * chunk, :]
            else:
                m = 1 << (lvl - 1)
                ref = jnp.concatenate(
                    [jnp.broadcast_to(bcum[p0 + m - 1:p0 + m, :], (2 * m, LANES))
                     for p0 in range(0, chunk, 2 * m)], axis=0)
                expo = (bcum - ref) * signs_ref[lvl - 1 - n_sum_levels]
            dec = jnp.exp(expo)
            a_l = lax.dot_general((q * dec).astype(_BF16), (kk * dec).astype(_BF16), _NT,
                                  preferred_element_type=_F32)
            a = a + a_l * masks_ref[lvl]

        b_last = bcum[chunk - 1:chunk, :]
        st = state[hh]
        inter = lax.dot_general((q * jnp.exp(bcum)).astype(_BF16), st.astype(_BF16), _NT,
                                preferred_element_type=_F32)
        o = inter + jnp.dot(a.astype(_BF16), v, preferred_element_type=_F32)

        k_dec = (kk * jnp.exp(b_last - bcum)).astype(_BF16)
        v_t = v.astype(_F32).T.astype(_BF16)
        state[hh] = st * jnp.exp(b_last) + jnp.dot(v_t, k_dec, preferred_element_type=_F32)

        o = o * lax.rsqrt(jnp.mean(o * o, axis=-1, keepdims=True) + EPS)
        y = o * gain * _silu(g_ref[0, rows, cols].astype(_F32))
        o_ref[0, rows, cols] = y.astype(o_ref.dtype)

    def chunk_step(ci, carry):
        rows = pl.ds(pl.multiple_of(ci * chunk, chunk), chunk)
        for hh in range(heads):
            head_chunk(hh, rows)
        return carry

    lax.fori_loop(0, tb // chunk, chunk_step, 0)


def _hgrn_call(q, fz, v, g, lb_logits, onorm, *, layer, tb=512, heads=16):
    b, s, w = q.shape
    nh = w // LANES
    tb = min(tb, s)
    chunk = min(HGRN_CHUNK, tb)
    assert s % tb == 0 and tb % chunk == 0 and nh % heads == 0
    sums, signs, masks = _hgrn_constants(chunk)
    sums = jnp.asarray(sums, _BF16)
    signs = jnp.asarray(signs, _F32)
    masks = jnp.asarray(masks, _F32)
    depth = lb_logits.shape[0]
    hw = heads * LANES
    head_tile = lambda bi, hi, si: (bi, si, hi)
    return pl.pallas_call(
        functools.partial(_hgrn_kernel, layer=layer, chunk=chunk, heads=heads),
        grid=(b, nh // heads, s // tb),
        in_specs=[pl.BlockSpec((1, tb, hw), head_tile),
                  pl.BlockSpec((1, tb, hw), head_tile),
                  pl.BlockSpec((1, tb, hw), head_tile),
                  pl.BlockSpec((1, tb, hw), head_tile),
                  pl.BlockSpec((depth, hw), lambda bi, hi, si: (0, hi)),
                  pl.BlockSpec((1, hw), lambda bi, hi, si: (0, hi)),
                  pl.BlockSpec(sums.shape, lambda bi, hi, si: (0, 0)),
                  pl.BlockSpec(signs.shape, lambda bi, hi, si: (0, 0, 0)),
                  pl.BlockSpec(masks.shape, lambda bi, hi, si: (0, 0, 0))],
        out_specs=pl.BlockSpec((1, tb, hw), head_tile),
        out_shape=jax.ShapeDtypeStruct((b, s, w), _BF16),
        scratch_shapes=[pltpu.VMEM((heads, LANES, LANES), _F32)],
        compiler_params=_params(("parallel", "parallel", "arbitrary")),
        name="hgrn2_recurrence",
    )(q, fz, v, g, lb_logits, onorm, sums, signs, masks)


def kernel(x, norm_gains, fox_w_in, fox_b_f, hgrn_w_in, hgrn_lb_logits, hgrn_onorm, w_out, final_gain):
    b, s, d = x.shape
    w = w_out.shape[1]
    nh = N_HEADS
    assert w == nh * LANES, "heads must be one lane tile wide"
    assert norm_gains.shape[0] == 2, "layer 0 is FoX, layer 1 is HGRN2"
    m = b * s
    x2 = x.reshape(m, d)

    wf = fox_w_in[0]
    wq, wk, wv = (wf[:, i * w:(i + 1) * w].astype(_BF16) for i in range(3))
    w_forget = jnp.pad(wf[:, 3 * w:3 * w + nh], ((0, 0), (0, LANES - nh))).astype(_BF16)
    wg = wf[:, 3 * w + nh:].astype(_BF16)
    q, k, v, g, flog = _proj_call(
        x2, norm_gains[0:1], [wq, wk, wv, wg], w_forget,
        ["scale", "none", "none", "none"], [_BF16] * 4, q_scale=float(LANES) ** -0.5 * LOG2E,
        name="fox_in_proj")
    b_pad = jnp.pad(fox_b_f[0:1], ((0, 0), (0, LANES - nh)))
    c = _decay_call(flog.reshape(b, s, LANES), b_pad)
    to3 = lambda t: t.reshape(b, s, w)
    y = _fox_attn_call(to3(q), to3(k), to3(v), to3(g), c)
    x2, h1 = _out_proj_call(y.reshape(m, w), w_out[0].astype(_BF16), x2, norm_gains[1:2],
                            emit_residual=True, norm_dtype=_BF16, name="fox_out_proj")

    wh = hgrn_w_in[0]
    whs = [wh[:, i * w:(i + 1) * w].astype(_BF16) for i in range(4)]
    q, fz, iv, g = _proj_call(h1, None, whs, None, ["silu", "none", "none", "none"],
                              [_BF16, _F32, _BF16, _BF16], name="hgrn_in_proj")
    y = _hgrn_call(to3(q), to3(fz), to3(iv), to3(g), hgrn_lb_logits, hgrn_onorm[0:1], layer=1)
    (out,) = _out_proj_call(y.reshape(m, w), w_out[1].astype(_BF16), x2, final_gain.reshape(1, d),
                            emit_residual=False, norm_dtype=x.dtype, name="hgrn_out_proj")
    return out.reshape(b, s, d)
```
